```python
import math
import jax, jax.numpy as jnp
from jax import lax
import numpy as np

D_MODEL = 2048
BATCH = 1
SEQ = 8192
DEPTH = 4

CHUNK = 64
N_A_LAYERS = DEPTH // 2
N_B_LAYERS = DEPTH - N_A_LAYERS
POOL_WINDOWS = (2, 4, 8, 16)
N_POOL_GROUPS = len(POOL_WINDOWS)
POOL_GROUP = D_MODEL // N_POOL_GROUPS
N_HEADS = 8
HEAD_DIM = D_MODEL // (2 * N_HEADS)
V_HEAD_DIM = 2 * HEAD_DIM
ROT_DIM = HEAD_DIM // 4
ROPE_THETA = 500000.0
D_FF = 4 * D_MODEL
Q_BLOCK = 128
EPS = 1e-6

kernel_name = "yoco_pool_diffattn_encoder"


def rms_norm(x, g):
    xf = x.astype(jnp.float32)
    y = xf * lax.rsqrt(jnp.mean(xf * xf, axis=-1, keepdims=True) + EPS)
    return (y * g.astype(jnp.float32)).astype(x.dtype)


def rope_tables(seq):
    pos = jnp.arange(seq, dtype=jnp.float32)
    inv_freq = ROPE_THETA ** (-jnp.arange(0, ROT_DIM, 2, dtype=jnp.float32) / ROT_DIM)
    ang = pos[:, None] * inv_freq[None, :]
    return jnp.cos(ang), jnp.sin(ang)


def apply_partial_rope(t, cos, sin):
    half = ROT_DIM // 2
    r1 = t[..., :half]
    r2 = t[..., half:ROT_DIM]
    keep = t[..., ROT_DIM:]
    c = cos[None, :, None, None, :].astype(t.dtype)
    s = sin[None, :, None, None, :].astype(t.dtype)
    return jnp.concatenate([r1 * c - r2 * s, r2 * c + r1 * s, keep], axis=-1)


def multiscale_pool_mixer(h, w_pool, scale):
    b, s, d = h.shape
    hf = h.astype(jnp.float32)
    t = jnp.arange(1, s + 1, dtype=jnp.float32)
    groups = []
    for g, w in enumerate(POOL_WINDOWS):
        hg = hf[..., g * POOL_GROUP:(g + 1) * POOL_GROUP]
        cs = jnp.cumsum(hg, axis=1)
        lagged = jnp.pad(cs, ((0, 0), (w, 0), (0, 0)))[:, :s]
        cnt = jnp.minimum(t, float(w))[None, :, None]
        groups.append((cs - lagged) / cnt - hg)
    pooled = jnp.stack(groups, axis=2).astype(h.dtype)
    y = jnp.einsum('bsgc,gce->bsge', pooled, w_pool).reshape(b, s, d)
    return y * scale


def shared_kv(x, g, w_kv, cos, sin):
    b, s, _ = x.shape
    h = rms_norm(x, g)
    kv = h @ w_kv
    k = kv[..., :D_MODEL].reshape(b, s, N_HEADS, 2, HEAD_DIM)
    v = kv[..., D_MODEL:].reshape(b, s, N_HEADS, V_HEAD_DIM)
    return apply_partial_rope(k, cos, sin), v


def diff_attention(h, k, v, w_q, lam, subln_g, w_o, cos, sin, layer_idx):
    b, s, _ = h.shape
    q = (h @ w_q).reshape(b, s, N_HEADS, 2, HEAD_DIM)
    q = apply_partial_rope(q, cos, sin)
    lam_init = 0.8 - 0.6 * math.exp(-0.3 * layer_idx)
    lamf = lam.astype(jnp.float32)
    lam_full = (jnp.exp(jnp.sum(lamf[0] * lamf[1]))
                - jnp.exp(jnp.sum(lamf[2] * lamf[3])) + lam_init)
    sm_scale = HEAD_DIM ** -0.5
    chunk_k = jnp.arange(s) // CHUNK
    vf = v.astype(jnp.float32)
    neg = jnp.finfo(jnp.float32).min

    def block(i):
        q0 = i * Q_BLOCK
        qb = lax.dynamic_slice_in_dim(q, q0, Q_BLOCK, axis=1)
        sc = jnp.einsum('bqhcd,bkhcd->bhcqk', qb, k,
                        preferred_element_type=jnp.float32) * sm_scale
        chunk_q = (q0 + jnp.arange(Q_BLOCK)) // CHUNK
        visible = chunk_k[None, :] <= chunk_q[:, None]
        sc = jnp.where(visible, sc, neg)
        p = jax.nn.softmax(sc, axis=-1)
        a = p[:, :, 0] - lam_full * p[:, :, 1]
        return jnp.einsum('bhqk,bkhe->bqhe', a, vf)

    o = lax.map(block, jnp.arange(s // Q_BLOCK))
    o = jnp.moveaxis(o, 0, 1).reshape(b, s, N_HEADS, V_HEAD_DIM)
    o = rms_norm(o, subln_g) * (1.0 - lam_init)
    return o.reshape(b, s, D_MODEL).astype(h.dtype) @ w_o


def sqrelu_mlp(h, w_in, w_out):
    u = jax.nn.relu(h @ w_in)
    return (u * u) @ w_out


def setup_inputs(seed: int = 0) -> dict:
    key = jax.random.key(seed)
    ks = jax.random.split(key, 16)
    f32 = jnp.float32
    nrm = lambda k, shape: jax.random.normal(k, shape, f32)
    x = nrm(ks[0], (BATCH, SEQ, D_MODEL))
    mix_norm = 1.0 + 0.02 * nrm(ks[1], (DEPTH, D_MODEL))
    mlp_norm = 1.0 + 0.02 * nrm(ks[2], (DEPTH, D_MODEL))
    pool_w = nrm(ks[3], (N_A_LAYERS, N_POOL_GROUPS, POOL_GROUP, POOL_GROUP)) * POOL_GROUP ** -0.5
    pool_scale = 1.0 + 0.02 * nrm(ks[4], (N_A_LAYERS, D_MODEL))
    kv_norm = 1.0 + 0.02 * nrm(ks[5], (D_MODEL,))
    w_kv = nrm(ks[6], (D_MODEL, 2 * D_MODEL)) * D_MODEL ** -0.5
    w_q = nrm(ks[7], (N_B_LAYERS, D_MODEL, D_MODEL)) * D_MODEL ** -0.5
    lam = 0.1 * nrm(ks[8], (N_B_LAYERS, 4, HEAD_DIM))
    subln = 1.0 + 0.02 * nrm(ks[9], (N_B_LAYERS, V_HEAD_DIM))
    w_o = nrm(ks[10], (N_B_LAYERS, D_MODEL, D_MODEL)) * D_MODEL ** -0.5
    w_mlp_in = nrm(ks[11], (DEPTH, D_MODEL, D_FF)) * D_MODEL ** -0.5
    w_mlp_out = nrm(ks[12], (DEPTH, D_FF, D_MODEL)) * D_FF ** -0.5
    final_norm = 1.0 + 0.02 * nrm(ks[13], (D_MODEL,))
    return {"x": x, "mix_norm": mix_norm, "mlp_norm": mlp_norm, "pool_w": pool_w,
            "pool_scale": pool_scale, "kv_norm": kv_norm, "w_kv": w_kv, "w_q": w_q,
            "lam": lam, "subln": subln, "w_o": w_o, "w_mlp_in": w_mlp_in,
            "w_mlp_out": w_mlp_out, "final_norm": final_norm}


def reference(x, mix_norm, mlp_norm, pool_w, pool_scale, kv_norm, w_kv, w_q, lam, subln,
              w_o, w_mlp_in, w_mlp_out, final_norm):
    s = x.shape[1]
    cos, sin = rope_tables(s)
    h = x
    k = None
    v = None
    for l in range(DEPTH):
        if l < N_A_LAYERS:
            h = h + multiscale_pool_mixer(rms_norm(h, mix_norm[l]), pool_w[l], pool_scale[l])
        else:
            j = l - N_A_LAYERS
            h = h + diff_attention(rms_norm(h, mix_norm[l]), k, v, w_q[j], lam[j], subln[j],
                                   w_o[j], cos, sin, l)
        h = h + sqrelu_mlp(rms_norm(h, mlp_norm[l]), w_mlp_in[l], w_mlp_out[l])
        if l == N_A_LAYERS - 1:
            k, v = shared_kv(h, kv_norm, w_kv, cos, sin)
    return rms_norm(h, final_norm)
```

```python
import functools
import math

import jax
import jax.numpy as jnp
import numpy as np
from jax import lax
from jax.experimental import pallas as pl
from jax.experimental.pallas import tpu as pltpu

D_MODEL = 2048
SEQ = 8192
DEPTH = 4
CHUNK = 64
N_A_LAYERS = DEPTH // 2
POOL_WINDOWS = (2, 4, 8, 16)
POOL_GROUP = D_MODEL // len(POOL_WINDOWS)
N_HEADS = 8
HEAD_DIM = D_MODEL // (2 * N_HEADS)
V_HEAD_DIM = 2 * HEAD_DIM
ROT_DIM = HEAD_DIM // 4
ROPE_THETA = 500000.0
D_FF = 4 * D_MODEL
EPS = 1e-6

LANES = 128
POOL_HALO = 16
MASK_VALUE = -1e30
VMEM_LIMIT = 56 * 1024 * 1024

F32 = jnp.float32
BF16 = jnp.bfloat16


def _rms(x, g):
    return x * lax.rsqrt(jnp.mean(x * x, axis=-1, keepdims=True) + EPS) * g


def _params(*sem):
    return pltpu.CompilerParams(dimension_semantics=sem, vmem_limit_bytes=VMEM_LIMIT)


def _mixer_kernel(h_ref, halo_ref, g_ref, w_ref, sc_ref, o_ref, ext_ref, *, tm):
    i = pl.program_id(0)
    g = g_ref[...]
    x = h_ref[...]
    n = _rms(x, g)
    nh = _rms(halo_ref[...], g)
    ext_ref[0:POOL_HALO, :] = jnp.where(i > 0, nh, 0.0)
    ext_ref[POOL_HALO:, :] = n
    row = i * tm + lax.broadcasted_iota(jnp.int32, (tm, 1), 0)
    for gi, w in enumerate(POOL_WINDOWS):
        cols = slice(gi * POOL_GROUP, (gi + 1) * POOL_GROUP)
        ng = n[:, cols]
        acc = ng
        for j in range(1, w):
            acc = acc + ext_ref[POOL_HALO - j:POOL_HALO - j + tm, cols]
        cnt = jnp.minimum(row + 1, w).astype(F32)
        pooled = acc / cnt - ng
        y = jnp.dot(pooled.astype(BF16), w_ref[gi], preferred_element_type=F32)
        o_ref[:, cols] = x[:, cols] + y * sc_ref[:, cols]


def _mixer(h, g, w_pool, scale, *, tm=512):
    s, d = h.shape
    return pl.pallas_call(
        functools.partial(_mixer_kernel, tm=tm),
        grid=(s // tm,),
        in_specs=[
            pl.BlockSpec((tm, d), lambda i: (i, 0)),
            pl.BlockSpec((POOL_HALO, d), lambda i: (jnp.maximum(i * (tm // POOL_HALO) - 1, 0), 0)),
            pl.BlockSpec((1, d), lambda i: (0, 0)),
            pl.BlockSpec(w_pool.shape, lambda i: (0, 0, 0)),
            pl.BlockSpec((1, d), lambda i: (0, 0)),
        ],
        out_specs=pl.BlockSpec((tm, d), lambda i: (i, 0)),
        out_shape=jax.ShapeDtypeStruct((s, d), F32),
        scratch_shapes=[pltpu.VMEM((tm + POOL_HALO, d), F32)],
        compiler_params=_params("arbitrary"),
        name="pool_mixer",
    )(h, h, g.reshape(1, d), w_pool, scale.reshape(1, d))


def _mlp_kernel(x_ref, g_ref, win_ref, wout_ref, fg_ref, o_ref, xn_ref, *, final):
    k = pl.program_id(1)

    @pl.when(k == 0)
    def _():
        x = x_ref[...]
        xn_ref[...] = _rms(x, g_ref[...]).astype(BF16)
        o_ref[...] = x

    u = jnp.dot(xn_ref[...], win_ref[...], preferred_element_type=F32)
    u = jnp.maximum(u, 0.0)
    u = (u * u).astype(BF16)
    o_ref[...] += jnp.dot(u, wout_ref[...], preferred_element_type=F32)

    if final:
        @pl.when(k == pl.num_programs(1) - 1)
        def _():
            o_ref[...] = _rms(o_ref[...], fg_ref[...])


def _mlp(h, g, w_in, w_out, final_g, *, final, tm=512, tf=512):
    s, d = h.shape
    f = w_in.shape[1]
    return pl.pallas_call(
        functools.partial(_mlp_kernel, final=final),
        grid=(s // tm, f // tf),
        in_specs=[
            pl.BlockSpec((tm, d), lambda i, k: (i, 0)),
            pl.BlockSpec((1, d), lambda i, k: (0, 0)),
            pl.BlockSpec((d, tf), lambda i, k: (0, k)),
            pl.BlockSpec((tf, d), lambda i, k: (k, 0)),
            pl.BlockSpec((1, d), lambda i, k: (0, 0)),
        ],
        out_specs=pl.BlockSpec((tm, d), lambda i, k: (i, 0)),
        out_shape=jax.ShapeDtypeStruct((s, d), F32),
        scratch_shapes=[pltpu.VMEM((tm, d), BF16)],
        compiler_params=_params("arbitrary", "arbitrary"),
        name="sqrelu_mlp",
    )(h, g.reshape(1, d), w_in, w_out, final_g.reshape(1, d))


def _proj_kernel(x_ref, g_ref, w_ref, c_ref, a_ref, b_ref, o_ref, xn_ref, *, n_rope_blocks, tn):
    j = pl.program_id(1)

    @pl.when(j == 0)
    def _():
        xn_ref[...] = _rms(x_ref[...], g_ref[...]).astype(BF16)

    y = jnp.dot(xn_ref[...], w_ref[...], preferred_element_type=F32)

    @pl.when(j < n_rope_blocks)
    def _():
        c = c_ref[...]
        a = a_ref[...]
        b = b_ref[...]
        for s in range(tn // LANES):
            t = y[:, s * LANES:(s + 1) * LANES]
            r = t * c + pltpu.roll(t, LANES - ROT_DIM // 2, 1) * a + pltpu.roll(t, ROT_DIM // 2, 1) * b
            o_ref[:, s * LANES:(s + 1) * LANES] = r.astype(BF16)

    @pl.when(j >= n_rope_blocks)
    def _():
        o_ref[...] = y.astype(BF16)


def _proj(h, g, w, tabs, *, n_rope_cols, tm=512, tn=512):
    s, d = h.shape
    n = w.shape[1]
    c, a, b = tabs
    tab_spec = pl.BlockSpec((tm, LANES), lambda i, j: (i, 0))
    return pl.pallas_call(
        functools.partial(_proj_kernel, n_rope_blocks=n_rope_cols // tn, tn=tn),
        grid=(s // tm, n // tn),
        in_specs=[
            pl.BlockSpec((tm, d), lambda i, j: (i, 0)),
            pl.BlockSpec((1, d), lambda i, j: (0, 0)),
            pl.BlockSpec((d, tn), lambda i, j: (0, j)),
            tab_spec, tab_spec, tab_spec,
        ],
        out_specs=pl.BlockSpec((tm, tn), lambda i, j: (i, j)),
        out_shape=jax.ShapeDtypeStruct((s, n), BF16),
        scratch_shapes=[pltpu.VMEM((tm, d), BF16)],
        compiler_params=_params("arbitrary", "arbitrary"),
        name="norm_proj_rope",
    )(h, g.reshape(1, d), w, c, a, b)


def _rope_tables(seq, scale):
    half = ROT_DIM // 2
    pos = jnp.arange(seq, dtype=F32)
    inv_freq = ROPE_THETA ** (-jnp.arange(0, ROT_DIM, 2, dtype=F32) / ROT_DIM)
    ang = pos[:, None] * inv_freq[None, :]
    cos, sin = jnp.cos(ang), jnp.sin(ang)
    zeros = jnp.zeros((seq, half), F32)
    keep = LANES - ROT_DIM
    c = jnp.concatenate([cos, cos, jnp.ones((seq, keep), F32)], axis=1)
    a = jnp.concatenate([-sin, zeros, jnp.zeros((seq, keep), F32)], axis=1)
    b = jnp.concatenate([zeros, sin, jnp.zeros((seq, keep), F32)], axis=1)
    return c * scale, a * scale, b * scale


def _attn_kernel(qi_tab, ki_tab, q_ref, k_ref, v_ref, lam_ref, sg_ref, o_ref,
                 m_ref, l_ref, acc_ref, *, tq, tk, lam_init):
    p_id = pl.program_id(1)
    qi = qi_tab[p_id]
    ki = ki_tab[p_id]

    @pl.when(ki == 0)
    def _():
        m_ref[...] = jnp.full(m_ref.shape, MASK_VALUE, F32)
        l_ref[...] = jnp.zeros(l_ref.shape, F32)
        acc_ref[...] = jnp.zeros(acc_ref.shape, F32)

    def step(masked):
        v = v_ref[...]
        if masked:
            rq = lax.broadcasted_iota(jnp.int32, (tq, tk), 0) // CHUNK
            ck = lax.broadcasted_iota(jnp.int32, (tq, tk), 1) // CHUNK
            visible = ck <= rq
        for c in range(2):
            q = q_ref[:, c * HEAD_DIM:(c + 1) * HEAD_DIM]
            k = k_ref[:, c * HEAD_DIM:(c + 1) * HEAD_DIM]
            s = lax.dot_general(q, k, (((1,), (1,)), ((), ())), preferred_element_type=F32)
            if masked:
                s = jnp.where(visible, s, MASK_VALUE)
            m_old = m_ref[c]
            m_new = jnp.maximum(m_old, jnp.max(s, axis=1, keepdims=True))
            alpha = jnp.exp(m_old - m_new)
            p = jnp.exp(s - m_new)
            l_ref[c] = alpha * l_ref[c] + jnp.sum(p, axis=1, keepdims=True)
            acc_ref[c] = alpha * acc_ref[c] + jnp.dot(p.astype(BF16), v, preferred_element_type=F32)
            m_ref[c] = m_new

    @pl.when(ki < qi)
    def _():
        step(False)

    @pl.when(ki == qi)
    def _():
        step(True)
        lam = lam_ref[...]
        lam_full = (jnp.exp(jnp.sum(lam[0:1] * lam[1:2], axis=1, keepdims=True))
                    - jnp.exp(jnp.sum(lam[2:3] * lam[3:4], axis=1, keepdims=True)) + lam_init)
        o = acc_ref[0] / l_ref[0] - lam_full * (acc_ref[1] / l_ref[1])
        o = _rms(o, sg_ref[...]) * (1.0 - lam_init)
        o_ref[...] = o.astype(BF16)


def _diff_attention(q, kv, lam, subln_g, *, lam_init, t=512):
    s = q.shape[0]
    nt = s // t
    pairs = [(qi, ki) for qi in range(nt) for ki in range(qi + 1)]
    qi_tab = jnp.asarray(np.array([p[0] for p in pairs], np.int32))
    ki_tab = jnp.asarray(np.array([p[1] for p in pairs], np.int32))
    v_off = D_MODEL // V_HEAD_DIM
    grid_spec = pltpu.PrefetchScalarGridSpec(
        num_scalar_prefetch=2,
        grid=(N_HEADS, len(pairs)),
        in_specs=[
            pl.BlockSpec((t, V_HEAD_DIM), lambda h, p, qt, kt: (qt[p], h)),
            pl.BlockSpec((t, V_HEAD_DIM), lambda h, p, qt, kt: (kt[p], h)),
            pl.BlockSpec((t, V_HEAD_DIM), lambda h, p, qt, kt: (kt[p], v_off + h)),
            pl.BlockSpec((4, HEAD_DIM), lambda h, p, qt, kt: (0, 0)),
            pl.BlockSpec((1, V_HEAD_DIM), lambda h, p, qt, kt: (0, 0)),
        ],
        out_specs=pl.BlockSpec((t, V_HEAD_DIM), lambda h, p, qt, kt: (qt[p], h)),
        scratch_shapes=[
            pltpu.VMEM((2, t, 1), F32),
            pltpu.VMEM((2, t, 1), F32),
            pltpu.VMEM((2, t, V_HEAD_DIM), F32),
        ],
    )
    return pl.pallas_call(
        functools.partial(_attn_kernel, tq=t, tk=t, lam_init=lam_init),
        grid_spec=grid_spec,
        out_shape=jax.ShapeDtypeStruct((s, D_MODEL), BF16),
        compiler_params=_params("arbitrary", "arbitrary"),
        name="diff_attention",
    )(qi_tab, ki_tab, q, kv, kv, lam, subln_g.reshape(1, V_HEAD_DIM))


def _oproj_kernel(o_ref, w_ref, h_ref, out_ref):
    out_ref[...] = h_ref[...] + jnp.dot(o_ref[...], w_ref[...], preferred_element_type=F32)


def _oproj(o, w, h, *, tm=512, tn=1024):
    s, d = h.shape
    return pl.pallas_call(
        _oproj_kernel,
        grid=(s // tm, d // tn),
        in_specs=[
            pl.BlockSpec((tm, d), lambda i, j: (i, 0)),
            pl.BlockSpec((d, tn), lambda i, j: (0, j)),
            pl.BlockSpec((tm, tn), lambda i, j: (i, j)),
        ],
        out_specs=pl.BlockSpec((tm, tn), lambda i, j: (i, j)),
        out_shape=jax.ShapeDtypeStruct((s, d), F32),
        compiler_params=_params("arbitrary", "arbitrary"),
        name="attn_out_proj",
    )(o, w, h)


def kernel(x, mix_norm, mlp_norm, pool_w, pool_scale, kv_norm, w_kv, w_q, lam, subln, w_o,
           w_mlp_in, w_mlp_out, final_norm):
    b, s, d = x.shape
    assert (b, s, d) == (1, SEQ, D_MODEL)
    h = x.reshape(s, d)
    k_tabs = _rope_tables(s, 1.0)
    q_tabs = _rope_tables(s, HEAD_DIM ** -0.5)
    kv = None
    for l in range(DEPTH):
        if l < N_A_LAYERS:
            h = _mixer(h, mix_norm[l], pool_w[l].astype(BF16), pool_scale[l])
        else:
            j = l - N_A_LAYERS
            lam_init = 0.8 - 0.6 * math.exp(-0.3 * l)
            q = _proj(h, mix_norm[l], w_q[j].astype(BF16), q_tabs, n_rope_cols=D_MODEL)
            o = _diff_attention(q, kv, lam[j], subln[j], lam_init=lam_init)
            h = _oproj(o, w_o[j].astype(BF16), h)
        h = _mlp(h, mlp_norm[l], w_mlp_in[l].astype(BF16), w_mlp_out[l].astype(BF16), final_norm,
                 final=(l == DEPTH - 1))
        if l == N_A_LAYERS - 1:
            kv = _proj(h, kv_norm, w_kv.astype(BF16), k_tabs, n_rope_cols=D_MODEL)
    return h.reshape(b, s, d)
```

```python
import functools
import math

import jax
import jax.numpy as jnp
import numpy as np
from jax import lax
from jax.experimental import pallas as pl
from jax.experimental.pallas import tpu as pltpu

D_MODEL = 2048
SEQ = 8192
DEPTH = 4
CHUNK = 64
N_A_LAYERS = DEPTH // 2
POOL_WINDOWS = (2, 4, 8, 16)
POOL_GROUP = D_MODEL // len(POOL_WINDOWS)
N_HEADS = 8
HEAD_DIM = D_MODEL // (2 * N_HEADS)
V_HEAD_DIM = 2 * HEAD_DIM
ROT_DIM = HEAD_DIM // 4
ROPE_THETA = 500000.0
D_FF = 4 * D_MODEL
EPS = 1e-6

LANES = 128
POOL_HALO = 16
MASK_VALUE = -1e30
VMEM_LIMIT = 56 * 1024 * 1024

F32 = jnp.float32
BF16 = jnp.bfloat16


def _rms(x, g):
    return x * lax.rsqrt(jnp.mean(x * x, axis=-1, keepdims=True) + EPS) * g


def _params(*sem):
    return pltpu.CompilerParams(dimension_semantics=sem, vmem_limit_bytes=VMEM_LIMIT)


def _mixer_kernel(h_ref, halo_ref, g_ref, w_ref, sc_ref, o_ref, ext_ref, *, tm):
    i = pl.program_id(0)
    g = g_ref[...]
    x = h_ref[...]
    n = _rms(x, g)
    nh = _rms(halo_ref[...], g)
    ext_ref[0:POOL_HALO, :] = jnp.where(i > 0, nh, 0.0)
    ext_ref[POOL_HALO:, :] = n
    row = i * tm + lax.broadcasted_iota(jnp.int32, (tm, 1), 0)
    for gi, w in enumerate(POOL_WINDOWS):
        cols = slice(gi * POOL_GROUP, (gi + 1) * POOL_GROUP)
        ng = n[:, cols]
        acc = ng
        for j in range(1, w):
            acc = acc + ext_ref[POOL_HALO - j:POOL_HALO - j + tm, cols]
        cnt = jnp.minimum(row + 1, w).astype(F32)
        pooled = acc / cnt - ng
        y = jnp.dot(pooled.astype(BF16), w_ref[gi], preferred_element_type=F32)
        o_ref[:, cols] = x[:, cols] + y * sc_ref[:, cols]


def _mixer(h, g, w_pool, layer, scale, *, tm=512):
    s, d = h.shape
    return pl.pallas_call(
        functools.partial(_mixer_kernel, tm=tm),
        grid=(s // tm,),
        in_specs=[
            pl.BlockSpec((tm, d), lambda i: (i, 0)),
            pl.BlockSpec((POOL_HALO, d), lambda i: (jnp.maximum(i * (tm // POOL_HALO) - 1, 0), 0)),
            pl.BlockSpec((1, d), lambda i: (0, 0)),
            pl.BlockSpec((None,) + w_pool.shape[1:], lambda i: (layer, 0, 0, 0)),
            pl.BlockSpec((1, d), lambda i: (0, 0)),
        ],
        out_specs=pl.BlockSpec((tm, d), lambda i: (i, 0)),
        out_shape=jax.ShapeDtypeStruct((s, d), F32),
        scratch_shapes=[pltpu.VMEM((tm + POOL_HALO, d), F32)],
        compiler_params=_params("arbitrary"),
        name="pool_mixer",
    )(h, h, g.reshape(1, d), w_pool, scale.reshape(1, d))


def _mlp_kernel(x_ref, g_ref, win_ref, wout_ref, fg_ref, o_ref, xn_ref, *, final):
    k = pl.program_id(1)

    @pl.when(k == 0)
    def _():
        x = x_ref[...]
        xn_ref[...] = _rms(x, g_ref[...]).astype(BF16)
        o_ref[...] = x

    u = jnp.dot(xn_ref[...], win_ref[...], preferred_element_type=F32)
    u = jnp.maximum(u, 0.0)
    u = (u * u).astype(BF16)
    o_ref[...] += jnp.dot(u, wout_ref[...], preferred_element_type=F32)

    if final:
        @pl.when(k == pl.num_programs(1) - 1)
        def _():
            o_ref[...] = _rms(o_ref[...], fg_ref[...])


def _mlp(h, g, w_in, w_out, layer, final_g, *, final, tm=512, tf=512):
    s, d = h.shape
    f = w_in.shape[2]
    return pl.pallas_call(
        functools.partial(_mlp_kernel, final=final),
        grid=(s // tm, f // tf),
        in_specs=[
            pl.BlockSpec((tm, d), lambda i, k: (i, 0)),
            pl.BlockSpec((1, d), lambda i, k: (0, 0)),
            pl.BlockSpec((None, d, tf), lambda i, k: (layer, 0, k)),
            pl.BlockSpec((None, tf, d), lambda i, k: (layer, k, 0)),
            pl.BlockSpec((1, d), lambda i, k: (0, 0)),
        ],
        out_specs=pl.BlockSpec((tm, d), lambda i, k: (i, 0)),
        out_shape=jax.ShapeDtypeStruct((s, d), F32),
        scratch_shapes=[pltpu.VMEM((tm, d), BF16)],
        compiler_params=_params("arbitrary", "arbitrary"),
        name="sqrelu_mlp",
    )(h, g.reshape(1, d), w_in, w_out, final_g.reshape(1, d))


def _proj_kernel(*refs, rope, transpose, tn):
    if rope:
        x_ref, g_ref, w_ref, c_ref, a_ref, b_ref, o_ref, xn_ref = refs
    else:
        x_ref, g_ref, w_ref, o_ref, xn_ref = refs
    j = pl.program_id(1)

    @pl.when(j == 0)
    def _():
        xn_ref[...] = _rms(x_ref[...], g_ref[...]).astype(BF16)

    y = jnp.dot(xn_ref[...], w_ref[...], preferred_element_type=F32)
    for s in range(tn // LANES):
        slab = slice(s * LANES, (s + 1) * LANES)
        t = y[:, slab]
        if rope:
            t = (t * c_ref[...] + pltpu.roll(t, LANES - ROT_DIM // 2, 1) * a_ref[...]
                 + pltpu.roll(t, ROT_DIM // 2, 1) * b_ref[...])
        if transpose:
            o_ref[slab, :] = t.T.astype(BF16)
        else:
            o_ref[:, slab] = t.astype(BF16)


def _proj(h, g, w, layer, tabs, *, col0, n, transpose, tm=512, tn=2048):
    s, d = h.shape
    rope = tabs is not None
    in_specs = [
        pl.BlockSpec((tm, d), lambda i, j: (i, 0)),
        pl.BlockSpec((1, d), lambda i, j: (0, 0)),
        pl.BlockSpec((None, d, tn), lambda i, j: (layer, 0, j + col0 // tn)),
    ]
    args = [h, g.reshape(1, d), w]
    if rope:
        in_specs += [pl.BlockSpec((tm, LANES), lambda i, j: (i, 0))] * 3
        args += list(tabs)
    if transpose:
        out_spec = pl.BlockSpec((tn, tm), lambda i, j: (j, i))
        out_shape = jax.ShapeDtypeStruct((n, s), BF16)
    else:
        out_spec = pl.BlockSpec((tm, tn), lambda i, j: (i, j))
        out_shape = jax.ShapeDtypeStruct((s, n), BF16)
    return pl.pallas_call(
        functools.partial(_proj_kernel, rope=rope, transpose=transpose, tn=tn),
        grid=(s // tm, n // tn),
        in_specs=in_specs,
        out_specs=out_spec,
        out_shape=out_shape,
        scratch_shapes=[pltpu.VMEM((tm, d), BF16)],
        compiler_params=_params("arbitrary", "arbitrary"),
        name="norm_proj",
    )(*args)


def _rope_tables(seq, scale):
    half = ROT_DIM // 2
    pos = jnp.arange(seq, dtype=F32)
    inv_freq = ROPE_THETA ** (-jnp.arange(0, ROT_DIM, 2, dtype=F32) / ROT_DIM)
    ang = pos[:, None] * inv_freq[None, :]
    cos, sin = jnp.cos(ang), jnp.sin(ang)
    zeros = jnp.zeros((seq, half), F32)
    keep = LANES - ROT_DIM
    c = jnp.concatenate([cos, cos, jnp.ones((seq, keep), F32)], axis=1)
    a = jnp.concatenate([-sin, zeros, jnp.zeros((seq, keep), F32)], axis=1)
    b = jnp.concatenate([zeros, sin, jnp.zeros((seq, keep), F32)], axis=1)
    return c * scale, a * scale, b * scale


def _attn_kernel(qi_tab, ki_tab, qT_ref, k_ref, vT_ref, lam_ref, sg_ref, o_ref,
                 m_ref, l_ref, acc_ref, *, t, sub, lam_init):
    p_id = pl.program_id(1)
    qi = qi_tab[p_id]
    ki = ki_tab[p_id]

    @pl.when(ki == 0)
    def _():
        m_ref[...] = jnp.full(m_ref.shape, MASK_VALUE, F32)
        l_ref[...] = jnp.zeros(l_ref.shape, F32)
        acc_ref[...] = jnp.zeros(acc_ref.shape, F32)

    def update(q0, nq, nk, masked):
        qs = slice(q0, q0 + nq)
        if masked:
            kc = lax.broadcasted_iota(jnp.int32, (nk, nq), 0) // CHUNK
            qc = (q0 + lax.broadcasted_iota(jnp.int32, (nk, nq), 1)) // CHUNK
            visible = kc <= qc
        for c in range(2):
            hd = slice(c * HEAD_DIM, (c + 1) * HEAD_DIM)
            s = jnp.dot(k_ref[0:nk, hd], qT_ref[hd, qs], preferred_element_type=F32)
            if masked:
                s = jnp.where(visible, s, MASK_VALUE)
            m_old = m_ref[c, :, qs]
            m_new = jnp.maximum(m_old, jnp.max(s, axis=0, keepdims=True))
            alpha = jnp.exp2(m_old - m_new)
            p = jnp.exp2(s - m_new)
            l_ref[c, :, qs] = alpha * l_ref[c, :, qs] + jnp.sum(p, axis=0, keepdims=True)
            pv = jnp.dot(vT_ref[:, 0:nk], p.astype(BF16), preferred_element_type=F32)
            acc_ref[c, :, qs] = alpha * acc_ref[c, :, qs] + pv
            m_ref[c, :, qs] = m_new

    @pl.when(ki < qi)
    def _():
        for q0 in range(0, t, sub):
            update(q0, sub, t, False)

    @pl.when(ki == qi)
    def _():
        for q0 in range(0, t, sub):
            update(q0, sub, q0 + sub, True)
        lam = lam_ref[...]
        lam_full = (jnp.exp(jnp.sum(lam[0:1] * lam[1:2], axis=1, keepdims=True))
                    - jnp.exp(jnp.sum(lam[2:3] * lam[3:4], axis=1, keepdims=True)) + lam_init)
        oT = acc_ref[0] / l_ref[0] - lam_full * (acc_ref[1] / l_ref[1])
        o = _rms(oT.T, sg_ref[...]) * (1.0 - lam_init)
        o_ref[...] = o.astype(BF16)


def _diff_attention(qT, k, vT, lam, subln_g, *, lam_init, t=1024, sub=512):
    s = k.shape[0]
    nt = s // t
    pairs = [(qi, ki) for qi in range(nt) for ki in range(qi + 1)]
    qi_tab = jnp.asarray(np.array([p[0] for p in pairs], np.int32))
    ki_tab = jnp.asarray(np.array([p[1] for p in pairs], np.int32))
    grid_spec = pltpu.PrefetchScalarGridSpec(
        num_scalar_prefetch=2,
        grid=(N_HEADS, len(pairs)),
        in_specs=[
            pl.BlockSpec((V_HEAD_DIM, t), lambda h, p, qt, kt: (h, qt[p])),
            pl.BlockSpec((t, V_HEAD_DIM), lambda h, p, qt, kt: (kt[p], h)),
            pl.BlockSpec((V_HEAD_DIM, t), lambda h, p, qt, kt: (h, kt[p])),
            pl.BlockSpec((4, HEAD_DIM), lambda h, p, qt, kt: (0, 0)),
            pl.BlockSpec((1, V_HEAD_DIM), lambda h, p, qt, kt: (0, 0)),
        ],
        out_specs=pl.BlockSpec((t, V_HEAD_DIM), lambda h, p, qt, kt: (qt[p], h)),
        scratch_shapes=[
            pltpu.VMEM((2, 1, t), F32),
            pltpu.VMEM((2, 1, t), F32),
            pltpu.VMEM((2, V_HEAD_DIM, t), F32),
        ],
    )
    return pl.pallas_call(
        functools.partial(_attn_kernel, t=t, sub=sub, lam_init=lam_init),
        grid_spec=grid_spec,
        out_shape=jax.ShapeDtypeStruct((s, D_MODEL), BF16),
        compiler_params=_params("arbitrary", "arbitrary"),
        name="diff_attention",
    )(qi_tab, ki_tab, qT, k, vT, lam, subln_g.reshape(1, V_HEAD_DIM))


def _oproj_kernel(o_ref, w_ref, h_ref, out_ref):
    out_ref[...] = h_ref[...] + jnp.dot(o_ref[...], w_ref[...], preferred_element_type=F32)


def _oproj(o, w, layer, h, *, tm=512, tn=2048):
    s, d = h.shape
    return pl.pallas_call(
        _oproj_kernel,
        grid=(s // tm, d // tn),
        in_specs=[
            pl.BlockSpec((tm, d), lambda i, j: (i, 0)),
            pl.BlockSpec((None, d, tn), lambda i, j: (layer, 0, j)),
            pl.BlockSpec((tm, tn), lambda i, j: (i, j)),
        ],
        out_specs=pl.BlockSpec((tm, tn), lambda i, j: (i, j)),
        out_shape=jax.ShapeDtypeStruct((s, d), F32),
        compiler_params=_params("arbitrary", "arbitrary"),
        name="attn_out_proj",
    )(o, w, h)


CAST_BLOCK_ELEMS = 2 * 1024 * 1024


def _cast_kernel(w_ref, o_ref):
    o_ref[...] = w_ref[...].astype(BF16)


def _to_bf16(w):
    cols = w.shape[-1]
    w2 = w.reshape(-1, cols)
    rows = w2.shape[0]
    rb = min(rows, CAST_BLOCK_ELEMS // cols)
    out = pl.pallas_call(
        _cast_kernel,
        grid=(rows // rb,),
        in_specs=[pl.BlockSpec((rb, cols), lambda i: (i, 0))],
        out_specs=pl.BlockSpec((rb, cols), lambda i: (i, 0)),
        out_shape=jax.ShapeDtypeStruct((rows, cols), BF16),
        compiler_params=_params("arbitrary"),
        name="cast_bf16",
    )(w2)
    return out.reshape(w.shape)


def kernel(x, mix_norm, mlp_norm, pool_w, pool_scale, kv_norm, w_kv, w_q, lam, subln, w_o,
           w_mlp_in, w_mlp_out, final_norm):
    b, s, d = x.shape
    assert (b, s, d) == (1, SEQ, D_MODEL)
    h = x.reshape(s, d)
    k_tabs = _rope_tables(s, 1.0)
    q_tabs = _rope_tables(s, HEAD_DIM ** -0.5 * math.log2(math.e))
    pool_w, w_q, w_o, w_mlp_in, w_mlp_out = map(_to_bf16, (pool_w, w_q, w_o, w_mlp_in, w_mlp_out))
    w_kv = _to_bf16(w_kv)[None]
    k = vT = None
    for l in range(DEPTH):
        if l < N_A_LAYERS:
            h = _mixer(h, mix_norm[l], pool_w, l, pool_scale[l])
        else:
            j = l - N_A_LAYERS
            lam_init = 0.8 - 0.6 * math.exp(-0.3 * l)
            qT = _proj(h, mix_norm[l], w_q, j, q_tabs, col0=0, n=D_MODEL, transpose=True)
            o = _diff_attention(qT, k, vT, lam[j], subln[j], lam_init=lam_init)
            h = _oproj(o, w_o, j, h)
        h = _mlp(h, mlp_norm[l], w_mlp_in, w_mlp_out, l, final_norm, final=(l == DEPTH - 1))
        if l == N_A_LAYERS - 1:
            k = _proj(h, kv_norm, w_kv, 0, k_tabs, col0=0, n=D_MODEL, transpose=False)
            vT = _proj(h, kv_norm, w_kv, 0, None, col0=D_MODEL, n=D_MODEL, transpose=True)
    return h.reshape(b, s, d)
```

```python
import functools
import math

import jax
import jax.numpy as jnp
import numpy as np
from jax import lax
from jax.experimental import pallas as pl
from jax.experimental.pallas import tpu as pltpu

D_MODEL = 2048
SEQ = 8192
DEPTH = 4
CHUNK = 64
N_A_LAYERS = DEPTH // 2
POOL_WINDOWS = (2, 4, 8, 16)
POOL_GROUP = D_MODEL // len(POOL_WINDOWS)
N_HEADS = 8
HEAD_DIM = D_MODEL // (2 * N_HEADS)
V_HEAD_DIM = 2 * HEAD_DIM
ROT_DIM = HEAD_DIM // 4
ROPE_THETA = 500000.0
D_FF = 4 * D_MODEL
EPS = 1e-6

LANES = 128
POOL_HALO = 16
MASK_VALUE = -1e30
VMEM_LIMIT = 56 * 1024 * 1024

F32 = jnp.float32
BF16 = jnp.bfloat16


def _rms(x, g):
    return x * lax.rsqrt(jnp.mean(x * x, axis=-1, keepdims=True) + EPS) * g


def _params(*sem, flags=None):
    return pltpu.CompilerParams(dimension_semantics=sem, vmem_limit_bytes=VMEM_LIMIT, flags=flags)


def _mixer_kernel(h_ref, halo_ref, g_ref, w_ref, sc_ref, o_ref, ext_ref, *, tm):
    i = pl.program_id(0)
    g = g_ref[...]
    x = h_ref[...]
    n = _rms(x, g)
    nh = _rms(halo_ref[...], g)
    ext_ref[0:POOL_HALO, :] = jnp.where(i > 0, nh, 0.0)
    ext_ref[POOL_HALO:, :] = n
    row = i * tm + lax.broadcasted_iota(jnp.int32, (tm, 1), 0)
    for gi, w in enumerate(POOL_WINDOWS):
        cols = slice(gi * POOL_GROUP, (gi + 1) * POOL_GROUP)
        ng = n[:, cols]
        acc = ng
        for j in range(1, w):
            acc = acc + ext_ref[POOL_HALO - j:POOL_HALO - j + tm, cols]
        cnt = jnp.minimum(row + 1, w).astype(F32)
        pooled = acc / cnt - ng
        y = jnp.dot(pooled.astype(BF16), w_ref[gi], preferred_element_type=F32)
        o_ref[:, cols] = x[:, cols] + y * sc_ref[:, cols]


def _mixer(h, g, w_pool, layer, scale, *, tm=512):
    s, d = h.shape
    return pl.pallas_call(
        functools.partial(_mixer_kernel, tm=tm),
        grid=(s // tm,),
        in_specs=[
            pl.BlockSpec((tm, d), lambda i: (i, 0)),
            pl.BlockSpec((POOL_HALO, d), lambda i: (jnp.maximum(i * (tm // POOL_HALO) - 1, 0), 0)),
            pl.BlockSpec((1, d), lambda i: (0, 0)),
            pl.BlockSpec((None,) + w_pool.shape[1:], lambda i: (layer, 0, 0, 0)),
            pl.BlockSpec((1, d), lambda i: (0, 0)),
        ],
        out_specs=pl.BlockSpec((tm, d), lambda i: (i, 0)),
        out_shape=jax.ShapeDtypeStruct((s, d), F32),
        scratch_shapes=[pltpu.VMEM((tm + POOL_HALO, d), F32)],
        compiler_params=_params("arbitrary"),
        name="pool_mixer",
    )(h, h, g.reshape(1, d), w_pool, scale.reshape(1, d))


MLP_PARTS = 2


def _mlp_kernel(x_ref, g_ref, win_ref, wout_ref, fg_ref, o_ref, xn_ref, *, final):
    k = pl.program_id(1)

    @pl.when(k == 0)
    def _():
        x = x_ref[...]
        xn_ref[...] = _rms(x, g_ref[...]).astype(BF16)
        o_ref[...] = x

    xn = xn_ref[...]
    part = win_ref.shape[1] // MLP_PARTS
    us = []
    for p in range(MLP_PARTS):
        u = jnp.dot(xn, win_ref[:, p * part:(p + 1) * part], preferred_element_type=F32)
        u = jnp.maximum(u, 0.0)
        us.append((u * u).astype(BF16))
    y = jnp.dot(us[0], wout_ref[0:part, :], preferred_element_type=F32)
    for p in range(1, MLP_PARTS):
        y += jnp.dot(us[p], wout_ref[p * part:(p + 1) * part, :], preferred_element_type=F32)
    o_ref[...] += y

    if final:
        @pl.when(k == pl.num_programs(1) - 1)
        def _():
            o_ref[...] = _rms(o_ref[...], fg_ref[...])


def _mlp(h, g, w_in, w_out, layer, final_g, *, final, tm=512, tf=1024):
    s, d = h.shape
    f = w_in.shape[2]
    return pl.pallas_call(
        functools.partial(_mlp_kernel, final=final),
        grid=(s // tm, f // tf),
        in_specs=[
            pl.BlockSpec((tm, d), lambda i, k: (i, 0)),
            pl.BlockSpec((1, d), lambda i, k: (0, 0)),
            pl.BlockSpec((None, d, tf), lambda i, k: (layer, 0, k)),
            pl.BlockSpec((None, tf, d), lambda i, k: (layer, k, 0)),
            pl.BlockSpec((1, d), lambda i, k: (0, 0)),
        ],
        out_specs=pl.BlockSpec((tm, d), lambda i, k: (i, 0)),
        out_shape=jax.ShapeDtypeStruct((s, d), F32),
        scratch_shapes=[pltpu.VMEM((tm, d), BF16)],
        compiler_params=_params("arbitrary", "arbitrary"),
        name="sqrelu_mlp",
    )(h, g.reshape(1, d), w_in, w_out, final_g.reshape(1, d))


def _proj_kernel(*refs, rope, transpose, tn):
    if rope:
        x_ref, g_ref, w_ref, c_ref, a_ref, b_ref, o_ref, xn_ref = refs
    else:
        x_ref, g_ref, w_ref, o_ref, xn_ref = refs
    j = pl.program_id(1)

    @pl.when(j == 0)
    def _():
        xn_ref[...] = _rms(x_ref[...], g_ref[...]).astype(BF16)

    y = jnp.dot(xn_ref[...], w_ref[...], preferred_element_type=F32)
    for s in range(tn // LANES):
        slab = slice(s * LANES, (s + 1) * LANES)
        t = y[:, slab]
        if rope:
            t = (t * c_ref[...] + pltpu.roll(t, LANES - ROT_DIM // 2, 1) * a_ref[...]
                 + pltpu.roll(t, ROT_DIM // 2, 1) * b_ref[...])
        if transpose:
            o_ref[slab, :] = t.T.astype(BF16)
        else:
            o_ref[:, slab] = t.astype(BF16)


def _proj(h, g, w, layer, tabs, *, col0, n, transpose, tm=512, tn=2048):
    s, d = h.shape
    rope = tabs is not None
    in_specs = [
        pl.BlockSpec((tm, d), lambda i, j: (i, 0)),
        pl.BlockSpec((1, d), lambda i, j: (0, 0)),
        pl.BlockSpec((None, d, tn), lambda i, j: (layer, 0, j + col0 // tn)),
    ]
    args = [h, g.reshape(1, d), w]
    if rope:
        in_specs += [pl.BlockSpec((tm, LANES), lambda i, j: (i, 0))] * 3
        args += list(tabs)
    if transpose:
        out_spec = pl.BlockSpec((tn, tm), lambda i, j: (j, i))
        out_shape = jax.ShapeDtypeStruct((n, s), BF16)
    else:
        out_spec = pl.BlockSpec((tm, tn), lambda i, j: (i, j))
        out_shape = jax.ShapeDtypeStruct((s, n), BF16)
    return pl.pallas_call(
        functools.partial(_proj_kernel, rope=rope, transpose=transpose, tn=tn),
        grid=(s // tm, n // tn),
        in_specs=in_specs,
        out_specs=out_spec,
        out_shape=out_shape,
        scratch_shapes=[pltpu.VMEM((tm, d), BF16)],
        compiler_params=_params("arbitrary", "arbitrary"),
        name="norm_proj",
    )(*args)


def _rope_tables(seq, scale):
    half = ROT_DIM // 2
    pos = jnp.arange(seq, dtype=F32)
    inv_freq = ROPE_THETA ** (-jnp.arange(0, ROT_DIM, 2, dtype=F32) / ROT_DIM)
    ang = pos[:, None] * inv_freq[None, :]
    cos, sin = jnp.cos(ang), jnp.sin(ang)
    zeros = jnp.zeros((seq, half), F32)
    keep = LANES - ROT_DIM
    c = jnp.concatenate([cos, cos, jnp.ones((seq, keep), F32)], axis=1)
    a = jnp.concatenate([-sin, zeros, jnp.zeros((seq, keep), F32)], axis=1)
    b = jnp.concatenate([zeros, sin, jnp.zeros((seq, keep), F32)], axis=1)
    return c * scale, a * scale, b * scale


def _attn_kernel(qi_tab, ki_tab, qT_ref, k_ref, vT_ref, lam_ref, sg_ref, o_ref,
                 m_ref, l_ref, acc_ref, s_ref, *, t, sub, kt, lam_init):
    p_id = pl.program_id(1)
    qi = qi_tab[p_id]
    ki = ki_tab[p_id]

    @pl.when(ki == 0)
    def _():
        m_ref[...] = jnp.full(m_ref.shape, MASK_VALUE, F32)
        l_ref[...] = jnp.zeros(l_ref.shape, F32)
        acc_ref[...] = jnp.zeros(acc_ref.shape, F32)

    def score_tile(chain, j, slot, masked):
        c, q0, _ = chain
        hd = slice(c * HEAD_DIM, (c + 1) * HEAD_DIM)
        rows = slice(j * kt, (j + 1) * kt)
        s = jnp.dot(k_ref[rows, hd], qT_ref[hd, q0:q0 + sub], preferred_element_type=F32)
        if masked and (j + 1) * kt > q0:
            kc = (j * kt + lax.broadcasted_iota(jnp.int32, (kt, sub), 0)) // CHUNK
            qc = (q0 + lax.broadcasted_iota(jnp.int32, (kt, sub), 1)) // CHUNK
            s = jnp.where(kc <= qc, s, MASK_VALUE)
        s_ref[slot, rows, :] = s
        return jnp.max(s, axis=0, keepdims=True)

    def value_tile(j, slot, m_new):
        rows = slice(j * kt, (j + 1) * kt)
        p = jnp.exp2(s_ref[slot, rows, :] - m_new)
        pv = jnp.dot(vT_ref[:, rows], p.astype(BF16), preferred_element_type=F32)
        return jnp.sum(p, axis=0, keepdims=True), pv

    def sweep(chains, masked):
        n_tiles = [nk // kt for _, _, nk in chains]
        maxes = [score_tile(chains[0], j, 0, masked) for j in range(n_tiles[0])]
        for i, (c, q0, _) in enumerate(chains):
            qs = slice(q0, q0 + sub)
            m_old = m_ref[c, :, qs]
            m_new = functools.reduce(jnp.maximum, maxes, m_old)
            alpha = jnp.exp2(m_old - m_new)
            nxt = i + 1 if i + 1 < len(chains) else None
            maxes, l_new, acc = [], alpha * l_ref[c, :, qs], None
            for j in range(max(n_tiles[i], n_tiles[nxt] if nxt else 0)):
                if nxt and j < n_tiles[nxt]:
                    maxes.append(score_tile(chains[nxt], j, nxt % 2, masked))
                if j < n_tiles[i]:
                    l_j, pv = value_tile(j, i % 2, m_new)
                    l_new = l_new + l_j
                    acc = alpha * acc_ref[c, :, qs] + pv if acc is None else acc + pv
            l_ref[c, :, qs] = l_new
            acc_ref[c, :, qs] = acc
            m_ref[c, :, qs] = m_new

    @pl.when(ki < qi)
    def _():
        sweep([(c, q0, t) for q0 in range(0, t, sub) for c in range(2)], False)

    @pl.when(ki == qi)
    def _():
        sweep([(c, q0, q0 + sub) for q0 in range(0, t, sub) for c in range(2)], True)
        lam = lam_ref[...]
        lam_full = (jnp.exp(jnp.sum(lam[0:1] * lam[1:2], axis=1, keepdims=True))
                    - jnp.exp(jnp.sum(lam[2:3] * lam[3:4], axis=1, keepdims=True)) + lam_init)
        oT = acc_ref[0] / l_ref[0] - lam_full * (acc_ref[1] / l_ref[1])
        o = _rms(oT.T, sg_ref[...]) * (1.0 - lam_init)
        o_ref[...] = o.astype(BF16)


def _diff_attention(qT, k, vT, lam, subln_g, *, lam_init, t=1024, sub=512, kt=256):
    s = k.shape[0]
    nt = s // t
    pairs = [(qi, ki) for qi in range(nt) for ki in range(qi + 1)]
    qi_tab = jnp.asarray(np.array([p[0] for p in pairs], np.int32))
    ki_tab = jnp.asarray(np.array([p[1] for p in pairs], np.int32))
    grid_spec = pltpu.PrefetchScalarGridSpec(
        num_scalar_prefetch=2,
        grid=(N_HEADS, len(pairs)),
        in_specs=[
            pl.BlockSpec((V_HEAD_DIM, t), lambda h, p, qt, kt: (h, qt[p])),
            pl.BlockSpec((t, V_HEAD_DIM), lambda h, p, qt, kt: (kt[p], h)),
            pl.BlockSpec((V_HEAD_DIM, t), lambda h, p, qt, kt: (h, kt[p])),
            pl.BlockSpec((4, HEAD_DIM), lambda h, p, qt, kt: (0, 0)),
            pl.BlockSpec((1, V_HEAD_DIM), lambda h, p, qt, kt: (0, 0)),
        ],
        out_specs=pl.BlockSpec((t, V_HEAD_DIM), lambda h, p, qt, kt: (qt[p], h)),
        scratch_shapes=[
            pltpu.VMEM((2, 1, t), F32),
            pltpu.VMEM((2, 1, t), F32),
            pltpu.VMEM((2, V_HEAD_DIM, t), F32),
            pltpu.VMEM((2, t, sub), F32),
        ],
    )
    return pl.pallas_call(
        functools.partial(_attn_kernel, t=t, sub=sub, kt=kt, lam_init=lam_init),
        grid_spec=grid_spec,
        out_shape=jax.ShapeDtypeStruct((s, D_MODEL), BF16),
        compiler_params=_params("arbitrary", "arbitrary"),
        name="diff_attention",
    )(qi_tab, ki_tab, qT, k, vT, lam, subln_g.reshape(1, V_HEAD_DIM))


def _oproj_kernel(o_ref, w_ref, h_ref, out_ref):
    out_ref[...] = h_ref[...] + jnp.dot(o_ref[...], w_ref[...], preferred_element_type=F32)


def _oproj(o, w, layer, h, *, tm=512, tn=2048):
    s, d = h.shape
    return pl.pallas_call(
        _oproj_kernel,
        grid=(s // tm, d // tn),
        in_specs=[
            pl.BlockSpec((tm, d), lambda i, j: (i, 0)),
            pl.BlockSpec((None, d, tn), lambda i, j: (layer, 0, j)),
            pl.BlockSpec((tm, tn), lambda i, j: (i, j)),
        ],
        out_specs=pl.BlockSpec((tm, tn), lambda i, j: (i, j)),
        out_shape=jax.ShapeDtypeStruct((s, d), F32),
        compiler_params=_params("arbitrary", "arbitrary"),
        name="attn_out_proj",
    )(o, w, h)


CAST_BLOCK_ELEMS = 2 * 1024 * 1024


def _cast_kernel(w_ref, o_ref):
    o_ref[...] = w_ref[...].astype(BF16)


def _to_bf16(w):
    cols = w.shape[-1]
    w2 = w.reshape(-1, cols)
    rows = w2.shape[0]
    rb = min(rows, CAST_BLOCK_ELEMS // cols)
    out = pl.pallas_call(
        _cast_kernel,
        grid=(rows // rb,),
        in_specs=[pl.BlockSpec((rb, cols), lambda i: (i, 0))],
        out_specs=pl.BlockSpec((rb, cols), lambda i: (i, 0)),
        out_shape=jax.ShapeDtypeStruct((rows, cols), BF16),
        compiler_params=_params("arbitrary"),
        name="cast_bf16",
    )(w2)
    return out.reshape(w.shape)


def kernel(x, mix_norm, mlp_norm, pool_w, pool_scale, kv_norm, w_kv, w_q, lam, subln, w_o,
           w_mlp_in, w_mlp_out, final_norm):
    b, s, d = x.shape
    assert (b, s, d) == (1, SEQ, D_MODEL)
    h = x.reshape(s, d)
    k_tabs = _rope_tables(s, 1.0)
    q_tabs = _rope_tables(s, HEAD_DIM ** -0.5 * math.log2(math.e))
    pool_w, w_q, w_o, w_mlp_in, w_mlp_out = map(_to_bf16, (pool_w, w_q, w_o, w_mlp_in, w_mlp_out))
    w_kv = _to_bf16(w_kv)[None]
    k = vT = None
    for l in range(DEPTH):
        if l < N_A_LAYERS:
            h = _mixer(h, mix_norm[l], pool_w, l, pool_scale[l])
        else:
            j = l - N_A_LAYERS
            lam_init = 0.8 - 0.6 * math.exp(-0.3 * l)
            qT = _proj(h, mix_norm[l], w_q, j, q_tabs, col0=0, n=D_MODEL, transpose=True)
            o = _diff_attention(qT, k, vT, lam[j], subln[j], lam_init=lam_init)
            h = _oproj(o, w_o, j, h)
        h = _mlp(h, mlp_norm[l], w_mlp_in, w_mlp_out, l, final_norm, final=(l == DEPTH - 1))
        if l == N_A_LAYERS - 1:
            k = _proj(h, kv_norm, w_kv, 0, k_tabs, col0=0, n=D_MODEL, transpose=False)
            vT = _proj(h, kv_norm, w_kv, 0, None, col0=D_MODEL, n=D_MODEL, transpose=True)
    return h.reshape(b, s, d)
```

```python
import functools
import math

import jax
import jax.numpy as jnp
from jax import lax
from jax.experimental import pallas as pl
from jax.experimental.pallas import tpu as pltpu

D_MODEL = 2048
SEQ = 8192
DEPTH = 4
CHUNK = 64
N_A_LAYERS = DEPTH // 2
POOL_WINDOWS = (2, 4, 8, 16)
POOL_GROUP = D_MODEL // len(POOL_WINDOWS)
N_HEADS = 8
HEAD_DIM = D_MODEL // (2 * N_HEADS)
V_HEAD_DIM = 2 * HEAD_DIM
ROT_DIM = HEAD_DIM // 4
ROPE_THETA = 500000.0
D_FF = 4 * D_MODEL
EPS = 1e-6

LANES = 128
POOL_HALO = 16
MASK_VALUE = -1e30
VMEM_LIMIT = 56 * 1024 * 1024

F32 = jnp.float32
BF16 = jnp.bfloat16


def _rms(x, g):
    return x * lax.rsqrt(jnp.mean(x * x, axis=-1, keepdims=True) + EPS) * g


def _params(*sem):
    return pltpu.CompilerParams(dimension_semantics=sem, vmem_limit_bytes=VMEM_LIMIT)


def _mixer_kernel(h_ref, halo_ref, g_ref, w_ref, sc_ref, o_ref, ext_ref, *, tm):
    i = pl.program_id(0)
    g = g_ref[...]
    x = h_ref[...]
    n = _rms(x, g)
    nh = _rms(halo_ref[...], g)
    ext_ref[0:POOL_HALO, :] = jnp.where(i > 0, nh, 0.0)
    ext_ref[POOL_HALO:, :] = n
    row = i * tm + lax.broadcasted_iota(jnp.int32, (tm, 1), 0)
    for gi, w in enumerate(POOL_WINDOWS):
        cols = slice(gi * POOL_GROUP, (gi + 1) * POOL_GROUP)
        ng = n[:, cols]
        acc = ng
        for j in range(1, w):
            acc = acc + ext_ref[POOL_HALO - j:POOL_HALO - j + tm, cols]
        cnt = jnp.minimum(row + 1, w).astype(F32)
        pooled = acc / cnt - ng
        y = jnp.dot(pooled.astype(BF16), w_ref[gi], preferred_element_type=F32)
        o_ref[:, cols] = x[:, cols] + y * sc_ref[:, cols]


def _mixer(h, g, w_pool, layer, scale, *, tm=512):
    s, d = h.shape
    return pl.pallas_call(
        functools.partial(_mixer_kernel, tm=tm),
        grid=(s // tm,),
        in_specs=[
            pl.BlockSpec((tm, d), lambda i: (i, 0)),
            pl.BlockSpec((POOL_HALO, d), lambda i: (jnp.maximum(i * (tm // POOL_HALO) - 1, 0), 0)),
            pl.BlockSpec((1, d), lambda i: (0, 0)),
            pl.BlockSpec((None,) + w_pool.shape[1:], lambda i: (layer, 0, 0, 0)),
            pl.BlockSpec((1, d), lambda i: (0, 0)),
        ],
        out_specs=pl.BlockSpec((tm, d), lambda i: (i, 0)),
        out_shape=jax.ShapeDtypeStruct((s, d), F32),
        scratch_shapes=[pltpu.VMEM((tm + POOL_HALO, d), F32)],
        compiler_params=_params("arbitrary"),
        name="pool_mixer",
    )(h, h, g.reshape(1, d), w_pool, scale.reshape(1, d))


MLP_PARTS = 2


def _mlp_kernel(x_ref, g_ref, win_ref, wout_ref, fg_ref, o_ref, xn_ref, *, final):
    k = pl.program_id(1)

    @pl.when(k == 0)
    def _():
        x = x_ref[...]
        xn_ref[...] = _rms(x, g_ref[...]).astype(BF16)
        o_ref[...] = x

    xn = xn_ref[...]
    part = win_ref.shape[1] // MLP_PARTS
    us = []
    for p in range(MLP_PARTS):
        u = jnp.dot(xn, win_ref[:, p * part:(p + 1) * part], preferred_element_type=F32)
        u = jnp.maximum(u, 0.0)
        us.append((u * u).astype(BF16))
    y = jnp.dot(us[0], wout_ref[0:part, :], preferred_element_type=F32)
    for p in range(1, MLP_PARTS):
        y += jnp.dot(us[p], wout_ref[p * part:(p + 1) * part, :], preferred_element_type=F32)
    o_ref[...] += y

    if final:
        @pl.when(k == pl.num_programs(1) - 1)
        def _():
            o_ref[...] = _rms(o_ref[...], fg_ref[...])


def _mlp(h, g, w_in, w_out, layer, final_g, *, final, tm=512, tf=1024):
    s, d = h.shape
    f = w_in.shape[2]
    return pl.pallas_call(
        functools.partial(_mlp_kernel, final=final),
        grid=(s // tm, f // tf),
        in_specs=[
            pl.BlockSpec((tm, d), lambda i, k: (i, 0)),
            pl.BlockSpec((1, d), lambda i, k: (0, 0)),
            pl.BlockSpec((None, d, tf), lambda i, k: (layer, 0, k)),
            pl.BlockSpec((None, tf, d), lambda i, k: (layer, k, 0)),
            pl.BlockSpec((1, d), lambda i, k: (0, 0)),
        ],
        out_specs=pl.BlockSpec((tm, d), lambda i, k: (i, 0)),
        out_shape=jax.ShapeDtypeStruct((s, d), F32),
        scratch_shapes=[pltpu.VMEM((tm, d), BF16)],
        compiler_params=_params("arbitrary", "arbitrary"),
        name="sqrelu_mlp",
    )(h, g.reshape(1, d), w_in, w_out, final_g.reshape(1, d))


def _proj_kernel(*refs, rope, transpose, key_tile, tn):
    if rope:
        x_ref, g_ref, w_ref, c_ref, a_ref, b_ref, o_ref, xn_ref = refs
    else:
        x_ref, g_ref, w_ref, o_ref, xn_ref = refs
    j = pl.program_id(1)

    @pl.when(j == 0)
    def _():
        xn_ref[...] = _rms(x_ref[...], g_ref[...]).astype(BF16)

    y = jnp.dot(xn_ref[...], w_ref[...], preferred_element_type=F32)
    for s in range(tn // LANES):
        slab = slice(s * LANES, (s + 1) * LANES)
        t = y[:, slab]
        if rope:
            t = (t * c_ref[...] + pltpu.roll(t, LANES - ROT_DIM // 2, 1) * a_ref[...]
                 + pltpu.roll(t, ROT_DIM // 2, 1) * b_ref[...])
        if not transpose:
            o_ref[:, slab] = t.astype(BF16)
        elif key_tile is None:
            o_ref[slab, :] = t.T.astype(BF16)
        else:
            tt = t.T.astype(BF16)
            for r in range(o_ref.shape[0]):
                o_ref[r, slab, :] = tt[:, r * key_tile:(r + 1) * key_tile]


def _proj(h, g, w, layer, tabs, *, col0, n, transpose, key_tile=None, tm=512, tn=2048):
    s, d = h.shape
    rope = tabs is not None
    in_specs = [
        pl.BlockSpec((tm, d), lambda i, j: (i, 0)),
        pl.BlockSpec((1, d), lambda i, j: (0, 0)),
        pl.BlockSpec((None, d, tn), lambda i, j: (layer, 0, j + col0 // tn)),
    ]
    args = [h, g.reshape(1, d), w]
    if rope:
        in_specs += [pl.BlockSpec((tm, LANES), lambda i, j: (i, 0))] * 3
        args += list(tabs)
    if transpose and key_tile is not None:
        out_spec = pl.BlockSpec((tm // key_tile, tn, key_tile), lambda i, j: (i, j, 0))
        out_shape = jax.ShapeDtypeStruct((s // key_tile, n, key_tile), BF16)
    elif transpose:
        out_spec = pl.BlockSpec((tn, tm), lambda i, j: (j, i))
        out_shape = jax.ShapeDtypeStruct((n, s), BF16)
    else:
        out_spec = pl.BlockSpec((tm, tn), lambda i, j: (i, j))
        out_shape = jax.ShapeDtypeStruct((s, n), BF16)
    return pl.pallas_call(
        functools.partial(_proj_kernel, rope=rope, transpose=transpose, key_tile=key_tile, tn=tn),
        grid=(s // tm, n // tn),
        in_specs=in_specs,
        out_specs=out_spec,
        out_shape=out_shape,
        scratch_shapes=[pltpu.VMEM((tm, d), BF16)],
        compiler_params=_params("arbitrary", "arbitrary"),
        name="norm_proj",
    )(*args)


def _rope_tables(seq, scale):
    half = ROT_DIM // 2
    pos = jnp.arange(seq, dtype=F32)
    inv_freq = ROPE_THETA ** (-jnp.arange(0, ROT_DIM, 2, dtype=F32) / ROT_DIM)
    ang = pos[:, None] * inv_freq[None, :]
    cos, sin = jnp.cos(ang), jnp.sin(ang)
    zeros = jnp.zeros((seq, half), F32)
    keep = LANES - ROT_DIM
    c = jnp.concatenate([cos, cos, jnp.ones((seq, keep), F32)], axis=1)
    a = jnp.concatenate([-sin, zeros, jnp.zeros((seq, keep), F32)], axis=1)
    b = jnp.concatenate([zeros, sin, jnp.zeros((seq, keep), F32)], axis=1)
    return c * scale, a * scale, b * scale


ATTN_Q_TILE = 1024
ATTN_Q_SUB = 512
ATTN_KEY_TILE = 256
ATTN_SCORE_LEAD = 2


def _attn_kernel(qT_ref, k_ref, vT_ref, lam_ref, sg_ref, o_ref, m_ref, l_ref, acc_ref, s_ref, *, lam_init):
    t, sub, kt = ATTN_Q_TILE, ATTN_Q_SUB, ATTN_KEY_TILE
    tiles = t // kt
    qi = pl.program_id(1)

    m_ref[...] = jnp.full(m_ref.shape, MASK_VALUE, F32)
    l_ref[...] = jnp.zeros(l_ref.shape, F32)
    acc_ref[...] = jnp.zeros(acc_ref.shape, F32)

    def full_chains(kb):
        return [(c, q0, tiles, kb, False) for q0 in range(0, t, sub) for c in range(2)]

    def diag_chains():
        return [(c, q0, (q0 + sub) // kt, qi, True) for q0 in range(0, t, sub) for c in range(2)]

    def score_tile(chain, j, slot):
        c, q0, _, kb, masked = chain
        hd = slice(c * HEAD_DIM, (c + 1) * HEAD_DIM)
        k = k_ref[pl.ds(pl.multiple_of(kb * t + j * kt, kt), kt), hd]
        s = jnp.dot(k, qT_ref[hd, q0:q0 + sub], preferred_element_type=F32)
        if masked and (j + 1) * kt > q0:
            kc = (j * kt + lax.broadcasted_iota(jnp.int32, (kt, sub), 0)) // CHUNK
            qc = (q0 + lax.broadcasted_iota(jnp.int32, (kt, sub), 1)) // CHUNK
            s = jnp.where(kc <= qc, s, MASK_VALUE)
        s_ref[slot, j * kt:(j + 1) * kt, :] = s
        return jnp.max(s, axis=0, keepdims=True)

    def value_tile(chain, j, slot, m_new):
        kb = chain[3]
        p = jnp.exp2(s_ref[slot, j * kt:(j + 1) * kt, :] - m_new)
        pv = jnp.dot(vT_ref[kb * tiles + j], p.astype(BF16), preferred_element_type=F32)
        return jnp.sum(p, axis=0, keepdims=True), pv

    def sweep(chains, maxes, trailing=None):
        seq = chains + ([trailing] if trailing is not None else [])
        for i, chain in enumerate(chains):
            c, q0, n, _, _ = chain
            qs = slice(q0, q0 + sub)
            m_old = m_ref[c, :, qs]
            m_new = functools.reduce(jnp.maximum, maxes, m_old)
            alpha = jnp.exp2(m_old - m_new)
            nxt = seq[i + 1] if i + 1 < len(seq) else None
            n_nxt = nxt[2] if nxt is not None else 0
            maxes, l_new, acc = [], alpha * l_ref[c, :, qs], None
            for j in range(n + 1):
                lead_end = n_nxt if j == n else min(n_nxt, j + ATTN_SCORE_LEAD + 1)
                for js in range(len(maxes), lead_end):
                    maxes.append(score_tile(nxt, js, (i + 1) % 2))
                if j < n:
                    l_j, pv = value_tile(chain, j, i % 2, m_new)
                    l_new = l_new + l_j
                    acc = alpha * acc_ref[c, :, qs] + pv if acc is None else acc + pv
            l_ref[c, :, qs] = l_new
            acc_ref[c, :, qs] = acc
            m_ref[c, :, qs] = m_new
        return maxes

    def first_scores(chain):
        return [score_tile(chain, j, 0) for j in range(chain[2])]

    @pl.when(qi == 0)
    def _():
        chains = diag_chains()
        sweep(chains, first_scores(chains[0]))

    @pl.when(qi > 0)
    def _():
        def body(kb, maxes):
            return tuple(sweep(full_chains(kb), list(maxes), trailing=full_chains(kb + 1)[0]))

        maxes = lax.fori_loop(0, qi - 1, body, tuple(first_scores(full_chains(0)[0])))
        sweep(full_chains(qi - 1) + diag_chains(), list(maxes))

    lam = lam_ref[...]
    lam_full = (jnp.exp(jnp.sum(lam[0:1] * lam[1:2], axis=1, keepdims=True))
                - jnp.exp(jnp.sum(lam[2:3] * lam[3:4], axis=1, keepdims=True)) + lam_init)
    oT = acc_ref[0] / l_ref[0] - lam_full * (acc_ref[1] / l_ref[1])
    o = _rms(oT.T, sg_ref[...]) * (1.0 - lam_init)
    o_ref[...] = o.astype(BF16)


def _diff_attention(qT, k, vT, lam, subln_g, *, lam_init):
    s = k.shape[0]
    t, sub, kt = ATTN_Q_TILE, ATTN_Q_SUB, ATTN_KEY_TILE
    return pl.pallas_call(
        functools.partial(_attn_kernel, lam_init=lam_init),
        grid=(N_HEADS, s // t),
        in_specs=[
            pl.BlockSpec((V_HEAD_DIM, t), lambda h, i: (h, i)),
            pl.BlockSpec((s, V_HEAD_DIM), lambda h, i: (0, h)),
            pl.BlockSpec((s // kt, V_HEAD_DIM, kt), lambda h, i: (0, h, 0)),
            pl.BlockSpec((4, HEAD_DIM), lambda h, i: (0, 0)),
            pl.BlockSpec((1, V_HEAD_DIM), lambda h, i: (0, 0)),
        ],
        out_specs=pl.BlockSpec((t, V_HEAD_DIM), lambda h, i: (i, h)),
        out_shape=jax.ShapeDtypeStruct((s, D_MODEL), BF16),
        scratch_shapes=[
            pltpu.VMEM((2, 1, t), F32),
            pltpu.VMEM((2, 1, t), F32),
            pltpu.VMEM((2, V_HEAD_DIM, t), F32),
            pltpu.VMEM((2, t, sub), F32),
        ],
        compiler_params=_params("arbitrary", "arbitrary"),
        name="diff_attention",
    )(qT, k, vT, lam, subln_g.reshape(1, V_HEAD_DIM))


def _oproj_kernel(o_ref, w_ref, h_ref, out_ref):
    out_ref[...] = h_ref[...] + jnp.dot(o_ref[...], w_ref[...], preferred_element_type=F32)


def _oproj(o, w, layer, h, *, tm=512, tn=2048):
    s, d = h.shape
    return pl.pallas_call(
        _oproj_kernel,
        grid=(s // tm, d // tn),
        in_specs=[
            pl.BlockSpec((tm, d), lambda i, j: (i, 0)),
            pl.BlockSpec((None, d, tn), lambda i, j: (layer, 0, j)),
            pl.BlockSpec((tm, tn), lambda i, j: (i, j)),
        ],
        out_specs=pl.BlockSpec((tm, tn), lambda i, j: (i, j)),
        out_shape=jax.ShapeDtypeStruct((s, d), F32),
        compiler_params=_params("arbitrary", "arbitrary"),
        name="attn_out_proj",
    )(o, w, h)


CAST_BLOCK_ELEMS = 2 * 1024 * 1024


def _cast_kernel(w_ref, o_ref):
    o_ref[...] = w_ref[...].astype(BF16)


def _to_bf16(w):
    cols = w.shape[-1]
    w2 = w.reshape(-1, cols)
    rows = w2.shape[0]
    rb = min(rows, CAST_BLOCK_ELEMS // cols)
    out = pl.pallas_call(
        _cast_kernel,
        grid=(rows // rb,),
        in_specs=[pl.BlockSpec((rb, cols), lambda i: (i, 0))],
        out_specs=pl.BlockSpec((rb, cols), lambda i: (i, 0)),
        out_shape=jax.ShapeDtypeStruct((rows, cols), BF16),
        compiler_params=_params("arbitrary"),
        name="cast_bf16",
    )(w2)
    return out.reshape(w.shape)


def kernel(x, mix_norm, mlp_norm, pool_w, pool_scale, kv_norm, w_kv, w_q, lam, subln, w_o,
           w_mlp_in, w_mlp_out, final_norm):
    b, s, d = x.shape
    assert (b, s, d) == (1, SEQ, D_MODEL)
    h = x.reshape(s, d)
    k_tabs = _rope_tables(s, 1.0)
    q_tabs = _rope_tables(s, HEAD_DIM ** -0.5 * math.log2(math.e))
    pool_w, w_q, w_o, w_mlp_in, w_mlp_out = map(_to_bf16, (pool_w, w_q, w_o, w_mlp_in, w_mlp_out))
    w_kv = _to_bf16(w_kv)[None]
    k = vT = None
    for l in range(DEPTH):
        if l < N_A_LAYERS:
            h = _mixer(h, mix_norm[l], pool_w, l, pool_scale[l])
        else:
            j = l - N_A_LAYERS
            lam_init = 0.8 - 0.6 * math.exp(-0.3 * l)
            qT = _proj(h, mix_norm[l], w_q, j, q_tabs, col0=0, n=D_MODEL, transpose=True)
            o = _diff_attention(qT, k, vT, lam[j], subln[j], lam_init=lam_init)
            h = _oproj(o, w_o, j, h)
        h = _mlp(h, mlp_norm[l], w_mlp_in, w_mlp_out, l, final_norm, final=(l == DEPTH - 1))
        if l == N_A_LAYERS - 1:
            k = _proj(h, kv_norm, w_kv, 0, k_tabs, col0=0, n=D_MODEL, transpose=False)
            vT = _proj(h, kv_norm, w_kv, 0, None, col0=D_MODEL, n=D_MODEL, transpose=True,
                       key_tile=ATTN_KEY_TILE)
    return h.reshape(b, s, d)
```

```python
import functools
import math

import jax
import jax.numpy as jnp
from jax import lax
from jax.experimental import pallas as pl
from jax.experimental.pallas import tpu as pltpu

D_MODEL = 2048
SEQ = 8192
DEPTH = 4
CHUNK = 64
N_A_LAYERS = DEPTH // 2
POOL_WINDOWS = (2, 4, 8, 16)
POOL_GROUP = D_MODEL // len(POOL_WINDOWS)
N_HEADS = 8
HEAD_DIM = D_MODEL // (2 * N_HEADS)
V_HEAD_DIM = 2 * HEAD_DIM
ROT_DIM = HEAD_DIM // 4
ROPE_THETA = 500000.0
D_FF = 4 * D_MODEL
EPS = 1e-6

LANES = 128
POOL_HALO = 16
MASK_VALUE = -1e30
VMEM_LIMIT = 56 * 1024 * 1024

F32 = jnp.float32
BF16 = jnp.bfloat16


def _rms(x, g):
    return x * lax.rsqrt(jnp.mean(x * x, axis=-1, keepdims=True) + EPS) * g


def _params(*sem):
    return pltpu.CompilerParams(dimension_semantics=sem, vmem_limit_bytes=VMEM_LIMIT)


def _mixer_kernel(h_ref, halo_ref, g_ref, w_ref, sc_ref, o_ref, ext_ref, *, tm):
    i = pl.program_id(0)
    g = g_ref[...]
    x = h_ref[...]
    n = _rms(x, g)
    nh = _rms(halo_ref[...], g)
    ext_ref[0:POOL_HALO, :] = jnp.where(i > 0, nh, 0.0)
    ext_ref[POOL_HALO:, :] = n
    row = i * tm + lax.broadcasted_iota(jnp.int32, (tm, 1), 0)
    for gi, w in enumerate(POOL_WINDOWS):
        cols = slice(gi * POOL_GROUP, (gi + 1) * POOL_GROUP)
        ng = n[:, cols]
        acc = ng
        for j in range(1, w):
            acc = acc + ext_ref[POOL_HALO - j:POOL_HALO - j + tm, cols]
        cnt = jnp.minimum(row + 1, w).astype(F32)
        pooled = acc / cnt - ng
        y = jnp.dot(pooled.astype(BF16), w_ref[gi], preferred_element_type=F32)
        o_ref[:, cols] = x[:, cols] + y * sc_ref[:, cols]


def _mixer(h, g, w_pool, layer, scale, *, tm=512):
    s, d = h.shape
    return pl.pallas_call(
        functools.partial(_mixer_kernel, tm=tm),
        grid=(s // tm,),
        in_specs=[
            pl.BlockSpec((tm, d), lambda i: (i, 0)),
            pl.BlockSpec((POOL_HALO, d), lambda i: (jnp.maximum(i * (tm // POOL_HALO) - 1, 0), 0)),
            pl.BlockSpec((1, d), lambda i: (0, 0)),
            pl.BlockSpec((None,) + w_pool.shape[1:], lambda i: (layer, 0, 0, 0)),
            pl.BlockSpec((1, d), lambda i: (0, 0)),
        ],
        out_specs=pl.BlockSpec((tm, d), lambda i: (i, 0)),
        out_shape=jax.ShapeDtypeStruct((s, d), F32),
        scratch_shapes=[pltpu.VMEM((tm + POOL_HALO, d), F32)],
        compiler_params=_params("arbitrary"),
        name="pool_mixer",
    )(h, h, g.reshape(1, d), w_pool, scale.reshape(1, d))


MLP_PARTS = 2


def _mlp_kernel(*refs, final, n_side):
    x_ref, g_ref, win_ref, wout_ref, fg_ref = refs[:5]
    side_in = refs[5:5 + n_side]
    o_ref = refs[5 + n_side]
    side_out = refs[6 + n_side:6 + 2 * n_side]
    xn_ref = refs[6 + 2 * n_side]
    k = pl.program_id(1)

    @pl.when(k == 0)
    def _():
        x = x_ref[...]
        xn_ref[...] = _rms(x, g_ref[...]).astype(BF16)
        o_ref[...] = x

    xn = xn_ref[...]
    part = win_ref.shape[1] // MLP_PARTS
    us = []
    for p in range(MLP_PARTS):
        u = jnp.dot(xn, win_ref[:, p * part:(p + 1) * part], preferred_element_type=F32)
        u = jnp.maximum(u, 0.0)
        us.append((u * u).astype(BF16))
    y = jnp.dot(us[0], wout_ref[0:part, :], preferred_element_type=F32)
    for p in range(1, MLP_PARTS):
        y += jnp.dot(us[p], wout_ref[p * part:(p + 1) * part, :], preferred_element_type=F32)
    o_ref[...] += y

    for src, dst in zip(side_in, side_out):
        dst[...] = src[...].astype(BF16)

    if final:
        @pl.when(k == pl.num_programs(1) - 1)
        def _():
            o_ref[...] = _rms(o_ref[...], fg_ref[...])


def _mlp(h, g, w_in, w_out, final_g, side, *, final, tm=512, tf=1024):
    s, d = h.shape
    f = w_in.shape[1]
    nk = f // tf
    steps = (s // tm) * nk
    side_specs, side_out_specs, side_shapes = [], [], []
    for arr, layer in side:
        _, rows, cols = arr.shape
        rps = rows // steps
        side_specs.append(pl.BlockSpec((None, rps, cols), lambda i, k, layer=layer: (layer, i * nk + k, 0)))
        side_out_specs.append(pl.BlockSpec((rps, cols), lambda i, k: (i * nk + k, 0)))
        side_shapes.append(jax.ShapeDtypeStruct((rows, cols), BF16))
    outs = pl.pallas_call(
        functools.partial(_mlp_kernel, final=final, n_side=len(side)),
        grid=(s // tm, nk),
        in_specs=[
            pl.BlockSpec((tm, d), lambda i, k: (i, 0)),
            pl.BlockSpec((1, d), lambda i, k: (0, 0)),
            pl.BlockSpec((d, tf), lambda i, k: (0, k)),
            pl.BlockSpec((tf, d), lambda i, k: (k, 0)),
            pl.BlockSpec((1, d), lambda i, k: (0, 0)),
        ] + side_specs,
        out_specs=[pl.BlockSpec((tm, d), lambda i, k: (i, 0))] + side_out_specs,
        out_shape=[jax.ShapeDtypeStruct((s, d), F32)] + side_shapes,
        scratch_shapes=[pltpu.VMEM((tm, d), BF16)],
        compiler_params=_params("arbitrary", "arbitrary"),
        name="sqrelu_mlp",
    )(h, g.reshape(1, d), w_in, w_out, final_g.reshape(1, d), *[arr for arr, _ in side])
    return outs[0], outs[1:]


def _proj_kernel(*refs, rope, transpose, key_tile, tn):
    if rope:
        x_ref, g_ref, w_ref, c_ref, a_ref, b_ref, o_ref, xn_ref = refs
    else:
        x_ref, g_ref, w_ref, o_ref, xn_ref = refs
    j = pl.program_id(1)

    @pl.when(j == 0)
    def _():
        xn_ref[...] = _rms(x_ref[...], g_ref[...]).astype(BF16)

    y = jnp.dot(xn_ref[...], w_ref[...], preferred_element_type=F32)
    for s in range(tn // LANES):
        slab = slice(s * LANES, (s + 1) * LANES)
        t = y[:, slab]
        if rope:
            t = (t * c_ref[...] + pltpu.roll(t, LANES - ROT_DIM // 2, 1) * a_ref[...]
                 + pltpu.roll(t, ROT_DIM // 2, 1) * b_ref[...])
        if not transpose:
            o_ref[:, slab] = t.astype(BF16)
        elif key_tile is None:
            o_ref[slab, :] = t.T.astype(BF16)
        else:
            tt = t.T.astype(BF16)
            for r in range(o_ref.shape[0]):
                o_ref[r, slab, :] = tt[:, r * key_tile:(r + 1) * key_tile]


def _proj(h, g, w, layer, tabs, *, col0, n, transpose, key_tile=None, tm=512, tn=2048):
    s, d = h.shape
    rope = tabs is not None
    in_specs = [
        pl.BlockSpec((tm, d), lambda i, j: (i, 0)),
        pl.BlockSpec((1, d), lambda i, j: (0, 0)),
        pl.BlockSpec((None, d, tn), lambda i, j: (layer, 0, j + col0 // tn)),
    ]
    args = [h, g.reshape(1, d), w]
    if rope:
        in_specs += [pl.BlockSpec((tm, LANES), lambda i, j: (i, 0))] * 3
        args += list(tabs)
    if transpose and key_tile is not None:
        out_spec = pl.BlockSpec((tm // key_tile, tn, key_tile), lambda i, j: (i, j, 0))
        out_shape = jax.ShapeDtypeStruct((s // key_tile, n, key_tile), BF16)
    elif transpose:
        out_spec = pl.BlockSpec((tn, tm), lambda i, j: (j, i))
        out_shape = jax.ShapeDtypeStruct((n, s), BF16)
    else:
        out_spec = pl.BlockSpec((tm, tn), lambda i, j: (i, j))
        out_shape = jax.ShapeDtypeStruct((s, n), BF16)
    return pl.pallas_call(
        functools.partial(_proj_kernel, rope=rope, transpose=transpose, key_tile=key_tile, tn=tn),
        grid=(s // tm, n // tn),
        in_specs=in_specs,
        out_specs=out_spec,
        out_shape=out_shape,
        scratch_shapes=[pltpu.VMEM((tm, d), BF16)],
        compiler_params=_params("arbitrary", "arbitrary"),
        name="norm_proj",
    )(*args)


def _rope_tables(seq, scale):
    half = ROT_DIM // 2
    pos = jnp.arange(seq, dtype=F32)
    inv_freq = ROPE_THETA ** (-jnp.arange(0, ROT_DIM, 2, dtype=F32) / ROT_DIM)
    ang = pos[:, None] * inv_freq[None, :]
    cos, sin = jnp.cos(ang), jnp.sin(ang)
    zeros = jnp.zeros((seq, half), F32)
    keep = LANES - ROT_DIM
    c = jnp.concatenate([cos, cos, jnp.ones((seq, keep), F32)], axis=1)
    a = jnp.concatenate([-sin, zeros, jnp.zeros((seq, keep), F32)], axis=1)
    b = jnp.concatenate([zeros, sin, jnp.zeros((seq, keep), F32)], axis=1)
    return c * scale, a * scale, b * scale


ATTN_Q_TILE = 1024
ATTN_Q_SUB = 512
ATTN_KEY_TILE = 256
ATTN_SCORE_LEAD = 2


def _attn_kernel(qT_ref, k_ref, vT_ref, lam_ref, sg_ref, o_ref, m_ref, l_ref, acc_ref, s_ref, *, lam_init):
    t, sub, kt = ATTN_Q_TILE, ATTN_Q_SUB, ATTN_KEY_TILE
    tiles = t // kt
    qi = pl.program_id(1)

    m_ref[...] = jnp.full(m_ref.shape, MASK_VALUE, F32)
    l_ref[...] = jnp.zeros(l_ref.shape, F32)
    acc_ref[...] = jnp.zeros(acc_ref.shape, F32)

    def full_chains(kb):
        return [(c, q0, tiles, kb, False) for q0 in range(0, t, sub) for c in range(2)]

    def diag_chains():
        return [(c, q0, (q0 + sub) // kt, qi, True) for q0 in range(0, t, sub) for c in range(2)]

    def score_tile(chain, j, slot):
        c, q0, _, kb, masked = chain
        hd = slice(c * HEAD_DIM, (c + 1) * HEAD_DIM)
        k = k_ref[pl.ds(pl.multiple_of(kb * t + j * kt, kt), kt), hd]
        s = jnp.dot(k, qT_ref[hd, q0:q0 + sub], preferred_element_type=F32)
        if masked and (j + 1) * kt > q0:
            kc = (j * kt + lax.broadcasted_iota(jnp.int32, (kt, sub), 0)) // CHUNK
            qc = (q0 + lax.broadcasted_iota(jnp.int32, (kt, sub), 1)) // CHUNK
            s = jnp.where(kc <= qc, s, MASK_VALUE)
        s_ref[slot, j * kt:(j + 1) * kt, :] = s
        return jnp.max(s, axis=0, keepdims=True)

    def value_tile(chain, j, slot, m_new):
        kb = chain[3]
        p = jnp.exp2(s_ref[slot, j * kt:(j + 1) * kt, :] - m_new)
        pv = jnp.dot(vT_ref[kb * tiles + j], p.astype(BF16), preferred_element_type=F32)
        return jnp.sum(p, axis=0, keepdims=True), pv

    def sweep(chains, maxes, trailing=None):
        seq = chains + ([trailing] if trailing is not None else [])
        for i, chain in enumerate(chains):
            c, q0, n, _, _ = chain
            qs = slice(q0, q0 + sub)
            m_old = m_ref[c, :, qs]
            m_new = functools.reduce(jnp.maximum, maxes, m_old)
            alpha = jnp.exp2(m_old - m_new)
            nxt = seq[i + 1] if i + 1 < len(seq) else None
            n_nxt = nxt[2] if nxt is not None else 0
            maxes, l_new, acc = [], alpha * l_ref[c, :, qs], None
            for j in range(n + 1):
                lead_end = n_nxt if j == n else min(n_nxt, j + ATTN_SCORE_LEAD + 1)
                for js in range(len(maxes), lead_end):
                    maxes.append(score_tile(nxt, js, (i + 1) % 2))
                if j < n:
                    l_j, pv = value_tile(chain, j, i % 2, m_new)
                    l_new = l_new + l_j
                    acc = alpha * acc_ref[c, :, qs] + pv if acc is None else acc + pv
            l_ref[c, :, qs] = l_new
            acc_ref[c, :, qs] = acc
            m_ref[c, :, qs] = m_new
        return maxes

    def first_scores(chain):
        return [score_tile(chain, j, 0) for j in range(chain[2])]

    @pl.when(qi == 0)
    def _():
        chains = diag_chains()
        sweep(chains, first_scores(chains[0]))

    @pl.when(qi > 0)
    def _():
        def body(kb, maxes):
            return tuple(sweep(full_chains(kb), list(maxes), trailing=full_chains(kb + 1)[0]))

        maxes = lax.fori_loop(0, qi - 1, body, tuple(first_scores(full_chains(0)[0])))
        sweep(full_chains(qi - 1) + diag_chains(), list(maxes))

    lam = lam_ref[...]
    lam_full = (jnp.exp(jnp.sum(lam[0:1] * lam[1:2], axis=1, keepdims=True))
                - jnp.exp(jnp.sum(lam[2:3] * lam[3:4], axis=1, keepdims=True)) + lam_init)
    oT = acc_ref[0] / l_ref[0] - lam_full * (acc_ref[1] / l_ref[1])
    o = _rms(oT.T, sg_ref[...]) * (1.0 - lam_init)
    o_ref[...] = o.astype(BF16)


def _diff_attention(qT, k, vT, lam, subln_g, *, lam_init):
    s = k.shape[0]
    t, sub, kt = ATTN_Q_TILE, ATTN_Q_SUB, ATTN_KEY_TILE
    return pl.pallas_call(
        functools.partial(_attn_kernel, lam_init=lam_init),
        grid=(N_HEADS, s // t),
        in_specs=[
            pl.BlockSpec((V_HEAD_DIM, t), lambda h, i: (h, i)),
            pl.BlockSpec((s, V_HEAD_DIM), lambda h, i: (0, h)),
            pl.BlockSpec((s // kt, V_HEAD_DIM, kt), lambda h, i: (0, h, 0)),
            pl.BlockSpec((4, HEAD_DIM), lambda h, i: (0, 0)),
            pl.BlockSpec((1, V_HEAD_DIM), lambda h, i: (0, 0)),
        ],
        out_specs=pl.BlockSpec((t, V_HEAD_DIM), lambda h, i: (i, h)),
        out_shape=jax.ShapeDtypeStruct((s, D_MODEL), BF16),
        scratch_shapes=[
            pltpu.VMEM((2, 1, t), F32),
            pltpu.VMEM((2, 1, t), F32),
            pltpu.VMEM((2, V_HEAD_DIM, t), F32),
            pltpu.VMEM((2, t, sub), F32),
        ],
        compiler_params=_params("arbitrary", "arbitrary"),
        name="diff_attention",
    )(qT, k, vT, lam, subln_g.reshape(1, V_HEAD_DIM))


def _oproj_kernel(o_ref, w_ref, h_ref, out_ref):
    out_ref[...] = h_ref[...] + jnp.dot(o_ref[...], w_ref[...], preferred_element_type=F32)


def _oproj(o, w, layer, h, *, tm=512, tn=2048):
    s, d = h.shape
    return pl.pallas_call(
        _oproj_kernel,
        grid=(s // tm, d // tn),
        in_specs=[
            pl.BlockSpec((tm, d), lambda i, j: (i, 0)),
            pl.BlockSpec((None, d, tn), lambda i, j: (layer, 0, j)),
            pl.BlockSpec((tm, tn), lambda i, j: (i, j)),
        ],
        out_specs=pl.BlockSpec((tm, tn), lambda i, j: (i, j)),
        out_shape=jax.ShapeDtypeStruct((s, d), F32),
        compiler_params=_params("arbitrary", "arbitrary"),
        name="attn_out_proj",
    )(o, w, h)


CAST_BLOCK_ELEMS = 2 * 1024 * 1024


def _cast_kernel(w_ref, o_ref):
    o_ref[...] = w_ref[...].astype(BF16)


def _to_bf16(w, layer):
    _, rows, cols = w.shape
    rb = min(rows, CAST_BLOCK_ELEMS // cols)
    return pl.pallas_call(
        _cast_kernel,
        grid=(rows // rb,),
        in_specs=[pl.BlockSpec((None, rb, cols), lambda i: (layer, i, 0))],
        out_specs=pl.BlockSpec((rb, cols), lambda i: (i, 0)),
        out_shape=jax.ShapeDtypeStruct((rows, cols), BF16),
        compiler_params=_params("arbitrary"),
        name="cast_bf16",
    )(w)


def kernel(x, mix_norm, mlp_norm, pool_w, pool_scale, kv_norm, w_kv, w_q, lam, subln, w_o,
           w_mlp_in, w_mlp_out, final_norm):
    b, s, d = x.shape
    assert (b, s, d) == (1, SEQ, D_MODEL)
    h = x.reshape(s, d)
    k_tabs = _rope_tables(s, 1.0)
    q_tabs = _rope_tables(s, HEAD_DIM ** -0.5 * math.log2(math.e))
    n_b = DEPTH - N_A_LAYERS
    pool16 = _to_bf16(pool_w.reshape(1, -1, POOL_GROUP), 0).reshape(pool_w.shape)
    w_in16, w_out16 = _to_bf16(w_mlp_in, 0), _to_bf16(w_mlp_out, 0)
    side_by_layer = {
        0: [(w_kv[None], 0), (w_q.reshape(1, n_b * d, d), 0)],
        1: [(w_o.reshape(1, n_b * d, d), 0)],
    }
    w_kv16 = w_q16 = w_o16 = k = vT = None
    for l in range(DEPTH):
        if l < N_A_LAYERS:
            h = _mixer(h, mix_norm[l], pool16, l, pool_scale[l])
        else:
            j = l - N_A_LAYERS
            lam_init = 0.8 - 0.6 * math.exp(-0.3 * l)
            qT = _proj(h, mix_norm[l], w_q16, j, q_tabs, col0=0, n=D_MODEL, transpose=True)
            o = _diff_attention(qT, k, vT, lam[j], subln[j], lam_init=lam_init)
            h = _oproj(o, w_o16, j, h)
        side = [(w_mlp_in, l + 1), (w_mlp_out, l + 1)] if l + 1 < DEPTH else []
        side += side_by_layer.get(l, [])
        h, cast = _mlp(h, mlp_norm[l], w_in16, w_out16, final_norm, side, final=(l == DEPTH - 1))
        if l + 1 < DEPTH:
            w_in16, w_out16 = cast[0], cast[1]
        if l == 0:
            w_kv16, w_q16 = cast[2][None], cast[3].reshape(n_b, d, d)
        if l == 1:
            w_o16 = cast[2].reshape(n_b, d, d)
        if l == N_A_LAYERS - 1:
            k = _proj(h, kv_norm, w_kv16, 0, k_tabs, col0=0, n=D_MODEL, transpose=False)
            vT = _proj(h, kv_norm, w_kv16, 0, None, col0=D_MODEL, n=D_MODEL, transpose=True,
                       key_tile=ATTN_KEY_TILE)
    return h.reshape(b, s, d)
```

```python
import functools
import math

import jax
import jax.numpy as jnp
from jax import lax
from jax.experimental import pallas as pl
from jax.experimental.pallas import tpu as pltpu

D_MODEL = 2048
SEQ = 8192
DEPTH = 4
CHUNK = 64
N_A_LAYERS = DEPTH // 2
POOL_WINDOWS = (2, 4, 8, 16)
POOL_GROUP = D_MODEL // len(POOL_WINDOWS)
N_HEADS = 8
HEAD_DIM = D_MODEL // (2 * N_HEADS)
V_HEAD_DIM = 2 * HEAD_DIM
ROT_DIM = HEAD_DIM // 4
ROPE_THETA = 500000.0
D_FF = 4 * D_MODEL
EPS = 1e-6

LANES = 128
POOL_HALO = 16
MASK_VALUE = -1e30
VMEM_LIMIT = 56 * 1024 * 1024

F32 = jnp.float32
BF16 = jnp.bfloat16


def _rms(x, g):
    return x * lax.rsqrt(jnp.mean(x * x, axis=-1, keepdims=True) + EPS) * g


def _params(*sem):
    return pltpu.CompilerParams(dimension_semantics=sem, vmem_limit_bytes=VMEM_LIMIT)


def _mixer_kernel(h_ref, halo_ref, g_ref, w_ref, sc_ref, o_ref, ext_ref, *, tm):
    i = pl.program_id(0)
    g = g_ref[...]
    x = h_ref[...]
    n = _rms(x, g)
    nh = _rms(halo_ref[...], g)
    ext_ref[0:POOL_HALO, :] = jnp.where(i > 0, nh, 0.0)
    ext_ref[POOL_HALO:, :] = n
    row = i * tm + lax.broadcasted_iota(jnp.int32, (tm, 1), 0)
    for gi, w in enumerate(POOL_WINDOWS):
        cols = slice(gi * POOL_GROUP, (gi + 1) * POOL_GROUP)
        ng = n[:, cols]
        win = ext_ref[:, cols]
        sh = 1
        while sh < w:
            win = win + pltpu.roll(win, sh, 0)
            sh *= 2
        acc = win[POOL_HALO:, :]
        inv_cnt = 1.0 / jnp.minimum(row + 1, w).astype(F32)
        pooled = acc * inv_cnt - ng
        y = jnp.dot(pooled.astype(BF16), w_ref[gi], preferred_element_type=F32)
        o_ref[:, cols] = x[:, cols] + y * sc_ref[:, cols]


def _mixer(h, g, w_pool, layer, scale, *, tm=512):
    s, d = h.shape
    return pl.pallas_call(
        functools.partial(_mixer_kernel, tm=tm),
        grid=(s // tm,),
        in_specs=[
            pl.BlockSpec((tm, d), lambda i: (i, 0)),
            pl.BlockSpec((POOL_HALO, d), lambda i: (jnp.maximum(i * (tm // POOL_HALO) - 1, 0), 0)),
            pl.BlockSpec((1, d), lambda i: (0, 0)),
            pl.BlockSpec((None,) + w_pool.shape[1:], lambda i: (layer, 0, 0, 0)),
            pl.BlockSpec((1, d), lambda i: (0, 0)),
        ],
        out_specs=pl.BlockSpec((tm, d), lambda i: (i, 0)),
        out_shape=jax.ShapeDtypeStruct((s, d), F32),
        scratch_shapes=[pltpu.VMEM((tm + POOL_HALO, d), F32)],
        compiler_params=_params("arbitrary"),
        name="pool_mixer",
    )(h, h, g.reshape(1, d), w_pool, scale.reshape(1, d))


MLP_PARTS = 2


def _mlp_kernel(*refs, final, n_side):
    x_ref, g_ref, win_ref, wout_ref, fg_ref = refs[:5]
    side_in = refs[5:5 + n_side]
    o_ref = refs[5 + n_side]
    side_out = refs[6 + n_side:6 + 2 * n_side]
    xn_ref = refs[6 + 2 * n_side]
    k = pl.program_id(1)

    @pl.when(k == 0)
    def _():
        x = x_ref[...]
        xn_ref[...] = _rms(x, g_ref[...]).astype(BF16)
        o_ref[...] = x

    xn = xn_ref[...]
    part = win_ref.shape[1] // MLP_PARTS
    us = []
    for p in range(MLP_PARTS):
        u = jnp.dot(xn, win_ref[:, p * part:(p + 1) * part], preferred_element_type=F32)
        u = jnp.maximum(u, 0.0)
        us.append((u * u).astype(BF16))
    o_ref[...] += jnp.dot(jnp.concatenate(us, axis=1), wout_ref[...], preferred_element_type=F32)

    for src, dst in zip(side_in, side_out):
        dst[...] = src[...].astype(BF16)

    if final:
        @pl.when(k == pl.num_programs(1) - 1)
        def _():
            o_ref[...] = _rms(o_ref[...], fg_ref[...])


def _mlp(h, g, w_in, w_out, final_g, side, *, final, tm=512, tf=1024):
    s, d = h.shape
    f = w_in.shape[1]
    nk = f // tf
    steps = (s // tm) * nk
    side_specs, side_out_specs, side_shapes = [], [], []
    for arr, layer in side:
        _, rows, cols = arr.shape
        rps = rows // steps
        side_specs.append(pl.BlockSpec((None, rps, cols), lambda i, k, layer=layer: (layer, i * nk + k, 0)))
        side_out_specs.append(pl.BlockSpec((rps, cols), lambda i, k: (i * nk + k, 0)))
        side_shapes.append(jax.ShapeDtypeStruct((rows, cols), BF16))
    outs = pl.pallas_call(
        functools.partial(_mlp_kernel, final=final, n_side=len(side)),
        grid=(s // tm, nk),
        in_specs=[
            pl.BlockSpec((tm, d), lambda i, k: (i, 0)),
            pl.BlockSpec((1, d), lambda i, k: (0, 0)),
            pl.BlockSpec((d, tf), lambda i, k: (0, k)),
            pl.BlockSpec((tf, d), lambda i, k: (k, 0)),
            pl.BlockSpec((1, d), lambda i, k: (0, 0)),
        ] + side_specs,
        out_specs=[pl.BlockSpec((tm, d), lambda i, k: (i, 0))] + side_out_specs,
        out_shape=[jax.ShapeDtypeStruct((s, d), F32)] + side_shapes,
        scratch_shapes=[pltpu.VMEM((tm, d), BF16)],
        compiler_params=_params("arbitrary", "arbitrary"),
        name="sqrelu_mlp",
    )(h, g.reshape(1, d), w_in, w_out, final_g.reshape(1, d), *[arr for arr, _ in side])
    return outs[0], outs[1:]


def _proj_kernel(*refs, rope, transpose, key_tile, tn):
    if rope:
        x_ref, g_ref, w_ref, c_ref, a_ref, b_ref, o_ref, xn_ref = refs
    else:
        x_ref, g_ref, w_ref, o_ref, xn_ref = refs
    j = pl.program_id(1)

    @pl.when(j == 0)
    def _():
        xn_ref[...] = _rms(x_ref[...], g_ref[...]).astype(BF16)

    y = jnp.dot(xn_ref[...], w_ref[...], preferred_element_type=F32)
    for s in range(tn // LANES):
        slab = slice(s * LANES, (s + 1) * LANES)
        t = y[:, slab]
        if rope:
            t = (t * c_ref[...] + pltpu.roll(t, LANES - ROT_DIM // 2, 1) * a_ref[...]
                 + pltpu.roll(t, ROT_DIM // 2, 1) * b_ref[...])
        if not transpose:
            o_ref[:, slab] = t.astype(BF16)
        elif key_tile is None:
            o_ref[slab, :] = t.T.astype(BF16)
        else:
            tt = t.T.astype(BF16)
            for r in range(o_ref.shape[0]):
                o_ref[r, slab, :] = tt[:, r * key_tile:(r + 1) * key_tile]


def _proj(h, g, w, layer, tabs, *, col0, n, transpose, key_tile=None, tm=512, tn=2048):
    s, d = h.shape
    rope = tabs is not None
    in_specs = [
        pl.BlockSpec((tm, d), lambda i, j: (i, 0)),
        pl.BlockSpec((1, d), lambda i, j: (0, 0)),
        pl.BlockSpec((None, d, tn), lambda i, j: (layer, 0, j + col0 // tn)),
    ]
    args = [h, g.reshape(1, d), w]
    if rope:
        in_specs += [pl.BlockSpec((tm, LANES), lambda i, j: (i, 0))] * 3
        args += list(tabs)
    if transpose and key_tile is not None:
        out_spec = pl.BlockSpec((tm // key_tile, tn, key_tile), lambda i, j: (i, j, 0))
        out_shape = jax.ShapeDtypeStruct((s // key_tile, n, key_tile), BF16)
    elif transpose:
        out_spec = pl.BlockSpec((tn, tm), lambda i, j: (j, i))
        out_shape = jax.ShapeDtypeStruct((n, s), BF16)
    else:
        out_spec = pl.BlockSpec((tm, tn), lambda i, j: (i, j))
        out_shape = jax.ShapeDtypeStruct((s, n), BF16)
    return pl.pallas_call(
        functools.partial(_proj_kernel, rope=rope, transpose=transpose, key_tile=key_tile, tn=tn),
        grid=(s // tm, n // tn),
        in_specs=in_specs,
        out_specs=out_spec,
        out_shape=out_shape,
        scratch_shapes=[pltpu.VMEM((tm, d), BF16)],
        compiler_params=_params("arbitrary", "arbitrary"),
        name="norm_proj",
    )(*args)


def _rope_tables(seq, scale):
    half = ROT_DIM // 2
    pos = jnp.arange(seq, dtype=F32)
    inv_freq = ROPE_THETA ** (-jnp.arange(0, ROT_DIM, 2, dtype=F32) / ROT_DIM)
    ang = pos[:, None] * inv_freq[None, :]
    cos, sin = jnp.cos(ang), jnp.sin(ang)
    zeros = jnp.zeros((seq, half), F32)
    keep = LANES - ROT_DIM
    c = jnp.concatenate([cos, cos, jnp.ones((seq, keep), F32)], axis=1)
    a = jnp.concatenate([-sin, zeros, jnp.zeros((seq, keep), F32)], axis=1)
    b = jnp.concatenate([zeros, sin, jnp.zeros((seq, keep), F32)], axis=1)
    return c * scale, a * scale, b * scale


ATTN_Q_TILE = 1024
ATTN_Q_SUB = 512
ATTN_KEY_TILE = 512


def _attn_kernel(qT_ref, k_ref, vT_ref, lam_ref, sg_ref, o_ref, m_ref, l_ref, acc_ref, s_ref, *, lam_init):
    t, sub, kt = ATTN_Q_TILE, ATTN_Q_SUB, ATTN_KEY_TILE
    tiles = t // kt
    qi = pl.program_id(1)

    m_ref[...] = jnp.full(m_ref.shape, MASK_VALUE, F32)
    l_ref[...] = jnp.zeros(l_ref.shape, F32)
    acc_ref[...] = jnp.zeros(acc_ref.shape, F32)

    def full_chains(kb):
        return [(c, q0, tiles, kb, False) for q0 in range(0, t, sub) for c in range(2)]

    def diag_chains():
        return [(c, q0, (q0 + sub) // kt, qi, True) for q0 in range(0, t, sub) for c in range(2)]

    def score_tile(chain, j, slot):
        c, q0, _, kb, masked = chain
        hd = slice(c * HEAD_DIM, (c + 1) * HEAD_DIM)
        k = k_ref[pl.ds(pl.multiple_of(kb * t + j * kt, kt), kt), hd]
        s = jnp.dot(k, qT_ref[hd, q0:q0 + sub], preferred_element_type=F32)
        if masked and (j + 1) * kt > q0:
            kc = (j * kt + lax.broadcasted_iota(jnp.int32, (kt, sub), 0)) // CHUNK
            qc = (q0 + lax.broadcasted_iota(jnp.int32, (kt, sub), 1)) // CHUNK
            s = jnp.where(kc <= qc, s, MASK_VALUE)
        s_ref[slot, j * kt:(j + 1) * kt, :] = s
        return jnp.max(s, axis=0, keepdims=True)

    def value_tile(chain, j, slot, m_new):
        kb = chain[3]
        p = jnp.exp2(s_ref[slot, j * kt:(j + 1) * kt, :] - m_new)
        pv = jnp.dot(vT_ref[kb * tiles + j], p.astype(BF16), preferred_element_type=F32)
        return jnp.sum(p, axis=0, keepdims=True), pv

    def sweep(chains, maxes, lead, trailing=None):
        seq = chains + ([trailing] if trailing is not None else [])
        for i, chain in enumerate(chains):
            c, q0, n, _, _ = chain
            qs = slice(q0, q0 + sub)
            m_old = m_ref[c, :, qs]
            m_new = functools.reduce(jnp.maximum, maxes, m_old)
            alpha = jnp.exp2(m_old - m_new)
            nxt = seq[i + 1] if i + 1 < len(seq) else None
            n_nxt = nxt[2] if nxt is not None else 0
            maxes, l_new, acc = [], alpha * l_ref[c, :, qs], None
            for j in range(n + 1):
                lead_end = n_nxt if j == n else min(n_nxt, j + lead + 1)
                for js in range(len(maxes), lead_end):
                    maxes.append(score_tile(nxt, js, (i + 1) % 2))
                if j < n:
                    l_j, pv = value_tile(chain, j, i % 2, m_new)
                    l_new = l_new + l_j
                    acc = alpha * acc_ref[c, :, qs] + pv if acc is None else acc + pv
            l_ref[c, :, qs] = l_new
            acc_ref[c, :, qs] = acc
            m_ref[c, :, qs] = m_new
        return maxes

    def first_scores(chain):
        return [score_tile(chain, j, 0) for j in range(chain[2])]

    @pl.when(qi == 0)
    def _():
        chains = diag_chains()
        sweep(chains, first_scores(chains[0]), 1)

    @pl.when(qi > 0)
    def _():
        def body(kb, maxes):
            return tuple(sweep(full_chains(kb), list(maxes), 0, trailing=full_chains(kb + 1)[0]))

        maxes = lax.fori_loop(0, qi - 1, body, tuple(first_scores(full_chains(0)[0])))
        sweep(full_chains(qi - 1) + diag_chains(), list(maxes), 1)

    lam = lam_ref[...]
    lam_full = (jnp.exp(jnp.sum(lam[0:1] * lam[1:2], axis=1, keepdims=True))
                - jnp.exp(jnp.sum(lam[2:3] * lam[3:4], axis=1, keepdims=True)) + lam_init)
    oT = acc_ref[0] * (1.0 / l_ref[0]) - acc_ref[1] * (lam_full / l_ref[1])
    inv_rms = lax.rsqrt(jnp.mean(oT * oT, axis=0, keepdims=True) + EPS) * (1.0 - lam_init)
    o_ref[...] = (oT * inv_rms * sg_ref[...]).astype(BF16)


def _diff_attention(qT, k, vT, lam, subln_g, *, lam_init):
    s = k.shape[0]
    t, sub, kt = ATTN_Q_TILE, ATTN_Q_SUB, ATTN_KEY_TILE
    return pl.pallas_call(
        functools.partial(_attn_kernel, lam_init=lam_init),
        grid=(N_HEADS, s // t),
        in_specs=[
            pl.BlockSpec((V_HEAD_DIM, t), lambda h, i: (h, i)),
            pl.BlockSpec((s, V_HEAD_DIM), lambda h, i: (0, h)),
            pl.BlockSpec((s // kt, V_HEAD_DIM, kt), lambda h, i: (0, h, 0)),
            pl.BlockSpec((4, HEAD_DIM), lambda h, i: (0, 0)),
            pl.BlockSpec((V_HEAD_DIM, 1), lambda h, i: (0, 0)),
        ],
        out_specs=pl.BlockSpec((V_HEAD_DIM, t), lambda h, i: (h, i)),
        out_shape=jax.ShapeDtypeStruct((D_MODEL, s), BF16),
        scratch_shapes=[
            pltpu.VMEM((2, 1, t), F32),
            pltpu.VMEM((2, 1, t), F32),
            pltpu.VMEM((2, V_HEAD_DIM, t), F32),
            pltpu.VMEM((2, t, sub), F32),
        ],
        compiler_params=_params("arbitrary", "arbitrary"),
        name="diff_attention",
    )(qT, k, vT, lam, subln_g.reshape(V_HEAD_DIM, 1))


def _oproj_kernel(oT_ref, w_ref, h_ref, out_ref):
    y = lax.dot_general(oT_ref[...], w_ref[...], (((0,), (0,)), ((), ())), preferred_element_type=F32)
    out_ref[...] = h_ref[...] + y


def _oproj(o, w, layer, h, *, tm=512, tn=2048):
    s, d = h.shape
    return pl.pallas_call(
        _oproj_kernel,
        grid=(s // tm, d // tn),
        in_specs=[
            pl.BlockSpec((d, tm), lambda i, j: (0, i)),
            pl.BlockSpec((None, d, tn), lambda i, j: (layer, 0, j)),
            pl.BlockSpec((tm, tn), lambda i, j: (i, j)),
        ],
        out_specs=pl.BlockSpec((tm, tn), lambda i, j: (i, j)),
        out_shape=jax.ShapeDtypeStruct((s, d), F32),
        compiler_params=_params("arbitrary", "arbitrary"),
        name="attn_out_proj",
    )(o, w, h)


CAST_BLOCK_ELEMS = 2 * 1024 * 1024


def _cast_kernel(w_ref, o_ref):
    o_ref[...] = w_ref[...].astype(BF16)


def _to_bf16(w, layer):
    _, rows, cols = w.shape
    rb = min(rows, CAST_BLOCK_ELEMS // cols)
    return pl.pallas_call(
        _cast_kernel,
        grid=(rows // rb,),
        in_specs=[pl.BlockSpec((None, rb, cols), lambda i: (layer, i, 0))],
        out_specs=pl.BlockSpec((rb, cols), lambda i: (i, 0)),
        out_shape=jax.ShapeDtypeStruct((rows, cols), BF16),
        compiler_params=_params("arbitrary"),
        name="cast_bf16",
    )(w)


def kernel(x, mix_norm, mlp_norm, pool_w, pool_scale, kv_norm, w_kv, w_q, lam, subln, w_o,
           w_mlp_in, w_mlp_out, final_norm):
    b, s, d = x.shape
    assert (b, s, d) == (1, SEQ, D_MODEL)
    h = x.reshape(s, d)
    k_tabs = _rope_tables(s, 1.0)
    q_tabs = _rope_tables(s, HEAD_DIM ** -0.5 * math.log2(math.e))
    n_b = DEPTH - N_A_LAYERS
    pool16 = _to_bf16(pool_w.reshape(1, -1, POOL_GROUP), 0).reshape(pool_w.shape)
    w_in16, w_out16 = _to_bf16(w_mlp_in, 0), _to_bf16(w_mlp_out, 0)
    side_by_layer = {
        0: [(w_kv[None], 0), (w_q.reshape(1, n_b * d, d), 0)],
        1: [(w_o.reshape(1, n_b * d, d), 0)],
    }
    w_kv16 = w_q16 = w_o16 = k = vT = None
    for l in range(DEPTH):
        if l < N_A_LAYERS:
            h = _mixer(h, mix_norm[l], pool16, l, pool_scale[l])
        else:
            j = l - N_A_LAYERS
            lam_init = 0.8 - 0.6 * math.exp(-0.3 * l)
            qT = _proj(h, mix_norm[l], w_q16, j, q_tabs, col0=0, n=D_MODEL, transpose=True)
            o = _diff_attention(qT, k, vT, lam[j], subln[j], lam_init=lam_init)
            h = _oproj(o, w_o16, j, h)
        side = [(w_mlp_in, l + 1), (w_mlp_out, l + 1)] if l + 1 < DEPTH else []
        side += side_by_layer.get(l, [])
        h, cast = _mlp(h, mlp_norm[l], w_in16, w_out16, final_norm, side, final=(l == DEPTH - 1))
        if l + 1 < DEPTH:
            w_in16, w_out16 = cast[0], cast[1]
        if l == 0:
            w_kv16, w_q16 = cast[2][None], cast[3].reshape(n_b, d, d)
        if l == 1:
            w_o16 = cast[2].reshape(n_b, d, d)
        if l == N_A_LAYERS - 1:
            k = _proj(h, kv_norm, w_kv16, 0, k_tabs, col0=0, n=D_MODEL, transpose=False)
            vT = _proj(h, kv_norm, w_kv16, 0, None, col0=D_MODEL, n=D_MODEL, transpose=True,
                       key_tile=ATTN_KEY_TILE)
    return h.reshape(b, s, d)
```

```python
import functools
import math

import jax
import jax.numpy as jnp
from jax import lax
from jax.experimental import pallas as pl
from jax.experimental.pallas import tpu as pltpu

D_MODEL = 2048
SEQ = 8192
DEPTH = 4
CHUNK = 64
N_A_LAYERS = DEPTH // 2
POOL_WINDOWS = (2, 4, 8, 16)
POOL_GROUP = D_MODEL // len(POOL_WINDOWS)
N_HEADS = 8
HEAD_DIM = D_MODEL // (2 * N_HEADS)
V_HEAD_DIM = 2 * HEAD_DIM
ROT_DIM = HEAD_DIM // 4
ROPE_THETA = 500000.0
D_FF = 4 * D_MODEL
EPS = 1e-6

LANES = 128
POOL_HALO = 16
MASK_VALUE = -1e30
VMEM_LIMIT = 56 * 1024 * 1024

F32 = jnp.float32
BF16 = jnp.bfloat16


def _rms(x, g):
    return x * lax.rsqrt(jnp.mean(x * x, axis=-1, keepdims=True) + EPS) * g


def _params(*sem):
    return pltpu.CompilerParams(dimension_semantics=sem, vmem_limit_bytes=VMEM_LIMIT)


def _mixer_kernel(h_ref, halo_ref, g_ref, w_ref, sc_ref, o_ref, ext_ref, *, tm):
    i = pl.program_id(0)
    g = g_ref[...]
    x = h_ref[...]
    n = _rms(x, g)
    nh = _rms(halo_ref[...], g)
    ext_ref[0:POOL_HALO, :] = jnp.where(i > 0, nh, 0.0)
    ext_ref[POOL_HALO:, :] = n
    row = i * tm + lax.broadcasted_iota(jnp.int32, (tm, 1), 0)
    for gi, w in enumerate(POOL_WINDOWS):
        cols = slice(gi * POOL_GROUP, (gi + 1) * POOL_GROUP)
        ng = n[:, cols]
        win = ext_ref[:, cols]
        sh = 1
        while sh < w:
            win = win + pltpu.roll(win, sh, 0)
            sh *= 2
        acc = win[POOL_HALO:, :]
        inv_cnt = 1.0 / jnp.minimum(row + 1, w).astype(F32)
        pooled = acc * inv_cnt - ng
        y = jnp.dot(pooled.astype(BF16), w_ref[gi], preferred_element_type=F32)
        o_ref[:, cols] = x[:, cols] + y * sc_ref[:, cols]


def _mixer(h, g, w_pool, layer, scale, *, tm=1024):
    s, d = h.shape
    return pl.pallas_call(
        functools.partial(_mixer_kernel, tm=tm),
        grid=(s // tm,),
        in_specs=[
            pl.BlockSpec((tm, d), lambda i: (i, 0)),
            pl.BlockSpec((POOL_HALO, d), lambda i: (jnp.maximum(i * (tm // POOL_HALO) - 1, 0), 0)),
            pl.BlockSpec((1, d), lambda i: (0, 0)),
            pl.BlockSpec((None,) + w_pool.shape[1:], lambda i: (layer, 0, 0, 0)),
            pl.BlockSpec((1, d), lambda i: (0, 0)),
        ],
        out_specs=pl.BlockSpec((tm, d), lambda i: (i, 0)),
        out_shape=jax.ShapeDtypeStruct((s, d), F32),
        scratch_shapes=[pltpu.VMEM((tm + POOL_HALO, d), F32)],
        compiler_params=_params("arbitrary"),
        name="pool_mixer",
    )(h, h, g.reshape(1, d), w_pool, scale.reshape(1, d))


MLP_PARTS = 2


def _mlp_kernel(*refs, final, n_side):
    x_ref, g_ref, win_ref, wout_ref, fg_ref = refs[:5]
    side_in = refs[5:5 + n_side]
    o_ref = refs[5 + n_side]
    side_out = refs[6 + n_side:6 + 2 * n_side]
    xn_ref = refs[6 + 2 * n_side]
    k = pl.program_id(1)

    @pl.when(k == 0)
    def _():
        x = x_ref[...]
        xn_ref[...] = _rms(x, g_ref[...]).astype(BF16)
        o_ref[...] = x

    xn = xn_ref[...]
    part = win_ref.shape[1] // MLP_PARTS
    us = []
    for p in range(MLP_PARTS):
        u = jnp.dot(xn, win_ref[:, p * part:(p + 1) * part], preferred_element_type=F32)
        u = jnp.maximum(u, 0.0)
        us.append((u * u).astype(BF16))
    o_ref[...] += jnp.dot(jnp.concatenate(us, axis=1), wout_ref[...], preferred_element_type=F32)

    for src, dst in zip(side_in, side_out):
        dst[...] = src[...].astype(BF16)

    if final:
        @pl.when(k == pl.num_programs(1) - 1)
        def _():
            o_ref[...] = _rms(o_ref[...], fg_ref[...])


def _mlp(h, g, w_in, w_out, final_g, side, *, final, tm=1024, tf=512):
    s, d = h.shape
    f = w_in.shape[1]
    nk = f // tf
    steps = (s // tm) * nk
    side_specs, side_out_specs, side_shapes = [], [], []
    for arr, layer in side:
        _, rows, cols = arr.shape
        rps = rows // steps
        side_specs.append(pl.BlockSpec((None, rps, cols), lambda i, k, layer=layer: (layer, i * nk + k, 0)))
        side_out_specs.append(pl.BlockSpec((rps, cols), lambda i, k: (i * nk + k, 0)))
        side_shapes.append(jax.ShapeDtypeStruct((rows, cols), BF16))
    outs = pl.pallas_call(
        functools.partial(_mlp_kernel, final=final, n_side=len(side)),
        grid=(s // tm, nk),
        in_specs=[
            pl.BlockSpec((tm, d), lambda i, k: (i, 0)),
            pl.BlockSpec((1, d), lambda i, k: (0, 0)),
            pl.BlockSpec((d, tf), lambda i, k: (0, k)),
            pl.BlockSpec((tf, d), lambda i, k: (k, 0)),
            pl.BlockSpec((1, d), lambda i, k: (0, 0)),
        ] + side_specs,
        out_specs=[pl.BlockSpec((tm, d), lambda i, k: (i, 0))] + side_out_specs,
        out_shape=[jax.ShapeDtypeStruct((s, d), F32)] + side_shapes,
        scratch_shapes=[pltpu.VMEM((tm, d), BF16)],
        compiler_params=_params("arbitrary", "arbitrary"),
        name="sqrelu_mlp",
    )(h, g.reshape(1, d), w_in, w_out, final_g.reshape(1, d), *[arr for arr, _ in side])
    return outs[0], outs[1:]


def _proj_kernel(x_ref, g_ref, w_ref, c_ref, a_ref, b_ref, *out_refs, modes):
    xn = _rms(x_ref[...], g_ref[...]).astype(BF16)
    n = D_MODEL
    for jj, (mode, o_ref) in enumerate(zip(modes, out_refs)):
        y = jnp.dot(xn, w_ref[:, jj * n:(jj + 1) * n], preferred_element_type=F32)
        for s in range(n // LANES):
            slab = slice(s * LANES, (s + 1) * LANES)
            t = y[:, slab]
            if mode != "tile_t":
                t = (t * c_ref[...] + pltpu.roll(t, LANES - ROT_DIM // 2, 1) * a_ref[...]
                     + pltpu.roll(t, ROT_DIM // 2, 1) * b_ref[...])
            if mode == "rope":
                o_ref[:, slab] = t.astype(BF16)
            elif mode == "rope_t":
                o_ref[slab, :] = t.T.astype(BF16)
            else:
                tt = t.T.astype(BF16)
                for r in range(o_ref.shape[0]):
                    o_ref[r, slab, :] = tt[:, r * ATTN_KEY_TILE:(r + 1) * ATTN_KEY_TILE]


def _proj(h, g, w, layer, tabs, modes, *, tm=512):
    s, d = h.shape
    n = D_MODEL
    out_specs, out_shapes = [], []
    for mode in modes:
        if mode == "rope":
            out_specs.append(pl.BlockSpec((tm, n), lambda i: (i, 0)))
            out_shapes.append(jax.ShapeDtypeStruct((s, n), BF16))
        elif mode == "rope_t":
            out_specs.append(pl.BlockSpec((n, tm), lambda i: (0, i)))
            out_shapes.append(jax.ShapeDtypeStruct((n, s), BF16))
        else:
            out_specs.append(pl.BlockSpec((tm // ATTN_KEY_TILE, n, ATTN_KEY_TILE), lambda i: (i, 0, 0)))
            out_shapes.append(jax.ShapeDtypeStruct((s // ATTN_KEY_TILE, n, ATTN_KEY_TILE), BF16))
    tab_spec = pl.BlockSpec((tm, LANES), lambda i: (i, 0))
    return pl.pallas_call(
        functools.partial(_proj_kernel, modes=tuple(modes)),
        grid=(s // tm,),
        in_specs=[
            pl.BlockSpec((tm, d), lambda i: (i, 0)),
            pl.BlockSpec((1, d), lambda i: (0, 0)),
            pl.BlockSpec((None, d, n * len(modes)), lambda i: (layer, 0, 0), pipeline_mode=pl.Buffered(1)),
            tab_spec, tab_spec, tab_spec,
        ],
        out_specs=out_specs,
        out_shape=out_shapes,
        compiler_params=_params("arbitrary"),
        name="norm_proj",
    )(h, g.reshape(1, d), w, *tabs)


def _rope_tables(seq, scale):
    half = ROT_DIM // 2
    pos = jnp.arange(seq, dtype=F32)
    inv_freq = ROPE_THETA ** (-jnp.arange(0, ROT_DIM, 2, dtype=F32) / ROT_DIM)
    ang = pos[:, None] * inv_freq[None, :]
    cos, sin = jnp.cos(ang), jnp.sin(ang)
    zeros = jnp.zeros((seq, half), F32)
    keep = LANES - ROT_DIM
    c = jnp.concatenate([cos, cos, jnp.ones((seq, keep), F32)], axis=1)
    a = jnp.concatenate([-sin, zeros, jnp.zeros((seq, keep), F32)], axis=1)
    b = jnp.concatenate([zeros, sin, jnp.zeros((seq, keep), F32)], axis=1)
    return c * scale, a * scale, b * scale


ATTN_Q_TILE = 1024
ATTN_Q_SUB = 512
ATTN_KEY_TILE = 512


def _attn_kernel(qT_ref, k_ref, vT_ref, lam_ref, sg_ref, o_ref, m_ref, l_ref, acc_ref, s_ref, *, lam_init):
    t, sub, kt = ATTN_Q_TILE, ATTN_Q_SUB, ATTN_KEY_TILE
    tiles = t // kt
    qi = pl.program_id(1)

    m_ref[...] = jnp.full(m_ref.shape, MASK_VALUE, F32)
    l_ref[...] = jnp.zeros(l_ref.shape, F32)
    acc_ref[...] = jnp.zeros(acc_ref.shape, F32)

    def full_chains(kb):
        return [(c, q0, tiles, kb, False) for q0 in range(0, t, sub) for c in range(2)]

    def diag_chains():
        return [(c, q0, (q0 + sub) // kt, qi, True) for q0 in range(0, t, sub) for c in range(2)]

    def score_tile(chain, j, slot):
        c, q0, _, kb, masked = chain
        hd = slice(c * HEAD_DIM, (c + 1) * HEAD_DIM)
        k = k_ref[pl.ds(pl.multiple_of(kb * t + j * kt, kt), kt), hd]
        s = jnp.dot(k, qT_ref[hd, q0:q0 + sub], preferred_element_type=F32)
        if masked and (j + 1) * kt > q0:
            kc = (j * kt + lax.broadcasted_iota(jnp.int32, (kt, sub), 0)) // CHUNK
            qc = (q0 + lax.broadcasted_iota(jnp.int32, (kt, sub), 1)) // CHUNK
            s = jnp.where(kc <= qc, s, MASK_VALUE)
        s_ref[slot, j * kt:(j + 1) * kt, :] = s
        return jnp.max(s, axis=0, keepdims=True)

    def value_tile(chain, j, slot, m_new):
        kb = chain[3]
        p = jnp.exp2(s_ref[slot, j * kt:(j + 1) * kt, :] - m_new)
        pv = jnp.dot(vT_ref[kb * tiles + j], p.astype(BF16), preferred_element_type=F32)
        return jnp.sum(p, axis=0, keepdims=True), pv

    def sweep(chains, maxes, lead, trailing=None):
        seq = chains + ([trailing] if trailing is not None else [])
        for i, chain in enumerate(chains):
            c, q0, n, _, _ = chain
            qs = slice(q0, q0 + sub)
            m_old = m_ref[c, :, qs]
            m_new = functools.reduce(jnp.maximum, maxes, m_old)
            alpha = jnp.exp2(m_old - m_new)
            nxt = seq[i + 1] if i + 1 < len(seq) else None
            n_nxt = nxt[2] if nxt is not None else 0
            maxes, l_new, acc = [], alpha * l_ref[c, :, qs], None
            for j in range(n + 1):
                lead_end = n_nxt if j == n else min(n_nxt, j + lead + 1)
                for js in range(len(maxes), lead_end):
                    maxes.append(score_tile(nxt, js, (i + 1) % 2))
                if j < n:
                    l_j, pv = value_tile(chain, j, i % 2, m_new)
                    l_new = l_new + l_j
                    acc = alpha * acc_ref[c, :, qs] + pv if acc is None else acc + pv
            l_ref[c, :, qs] = l_new
            acc_ref[c, :, qs] = acc
            m_ref[c, :, qs] = m_new
        return maxes

    def first_scores(chain):
        return [score_tile(chain, j, 0) for j in range(chain[2])]

    @pl.when(qi == 0)
    def _():
        chains = diag_chains()
        sweep(chains, first_scores(chains[0]), 1)

    @pl.when(qi > 0)
    def _():
        def body(kb, maxes):
            return tuple(sweep(full_chains(kb), list(maxes), 0, trailing=full_chains(kb + 1)[0]))

        maxes = lax.fori_loop(0, qi - 1, body, tuple(first_scores(full_chains(0)[0])))
        sweep(full_chains(qi - 1) + diag_chains(), list(maxes), 1)

    lam = lam_ref[...]
    lam_full = (jnp.exp(jnp.sum(lam[0:1] * lam[1:2], axis=1, keepdims=True))
                - jnp.exp(jnp.sum(lam[2:3] * lam[3:4], axis=1, keepdims=True)) + lam_init)
    oT = acc_ref[0] * (1.0 / l_ref[0]) - acc_ref[1] * (lam_full / l_ref[1])
    inv_rms = lax.rsqrt(jnp.mean(oT * oT, axis=0, keepdims=True) + EPS) * (1.0 - lam_init)
    o_ref[...] = (oT * inv_rms * sg_ref[...]).astype(BF16)


def _diff_attention(qT, k, vT, lam, subln_g, *, lam_init):
    s = k.shape[0]
    t, sub, kt = ATTN_Q_TILE, ATTN_Q_SUB, ATTN_KEY_TILE
    return pl.pallas_call(
        functools.partial(_attn_kernel, lam_init=lam_init),
        grid=(N_HEADS, s // t),
        in_specs=[
            pl.BlockSpec((V_HEAD_DIM, t), lambda h, i: (h, i)),
            pl.BlockSpec((s, V_HEAD_DIM), lambda h, i: (0, h)),
            pl.BlockSpec((s // kt, V_HEAD_DIM, kt), lambda h, i: (0, h, 0)),
            pl.BlockSpec((4, HEAD_DIM), lambda h, i: (0, 0)),
            pl.BlockSpec((V_HEAD_DIM, 1), lambda h, i: (0, 0)),
        ],
        out_specs=pl.BlockSpec((V_HEAD_DIM, t), lambda h, i: (h, i)),
        out_shape=jax.ShapeDtypeStruct((D_MODEL, s), BF16),
        scratch_shapes=[
            pltpu.VMEM((2, 1, t), F32),
            pltpu.VMEM((2, 1, t), F32),
            pltpu.VMEM((2, V_HEAD_DIM, t), F32),
            pltpu.VMEM((2, t, sub), F32),
        ],
        compiler_params=_params("arbitrary", "arbitrary"),
        name="diff_attention",
    )(qT, k, vT, lam, subln_g.reshape(V_HEAD_DIM, 1))


def _oproj_kernel(oT_ref, w_ref, h_ref, out_ref):
    y = lax.dot_general(oT_ref[...], w_ref[...], (((0,), (0,)), ((), ())), preferred_element_type=F32)
    out_ref[...] = h_ref[...] + y


def _oproj(o, w, layer, h, *, tm=512, tn=2048):
    s, d = h.shape
    return pl.pallas_call(
        _oproj_kernel,
        grid=(s // tm, d // tn),
        in_specs=[
            pl.BlockSpec((d, tm), lambda i, j: (0, i)),
            pl.BlockSpec((None, d, tn), lambda i, j: (layer, 0, j)),
            pl.BlockSpec((tm, tn), lambda i, j: (i, j)),
        ],
        out_specs=pl.BlockSpec((tm, tn), lambda i, j: (i, j)),
        out_shape=jax.ShapeDtypeStruct((s, d), F32),
        compiler_params=_params("arbitrary", "arbitrary"),
        name="attn_out_proj",
    )(o, w, h)


CAST_BLOCK_ELEMS = 2 * 1024 * 1024


def _cast_kernel(w_ref, o_ref):
    o_ref[...] = w_ref[...].astype(BF16)


def _to_bf16(w, layer):
    _, rows, cols = w.shape
    rb = min(rows, CAST_BLOCK_ELEMS // cols)
    return pl.pallas_call(
        _cast_kernel,
        grid=(rows // rb,),
        in_specs=[pl.BlockSpec((None, rb, cols), lambda i: (layer, i, 0))],
        out_specs=pl.BlockSpec((rb, cols), lambda i: (i, 0)),
        out_shape=jax.ShapeDtypeStruct((rows, cols), BF16),
        compiler_params=_params("arbitrary"),
        name="cast_bf16",
    )(w)


def kernel(x, mix_norm, mlp_norm, pool_w, pool_scale, kv_norm, w_kv, w_q, lam, subln, w_o,
           w_mlp_in, w_mlp_out, final_norm):
    b, s, d = x.shape
    assert (b, s, d) == (1, SEQ, D_MODEL)
    h = x.reshape(s, d)
    k_tabs = _rope_tables(s, 1.0)
    q_tabs = _rope_tables(s, HEAD_DIM ** -0.5 * math.log2(math.e))
    n_b = DEPTH - N_A_LAYERS
    pool16 = _to_bf16(pool_w.reshape(1, -1, POOL_GROUP), 0).reshape(pool_w.shape)
    w_in16, w_out16 = _to_bf16(w_mlp_in, 0), _to_bf16(w_mlp_out, 0)
    side_by_layer = {
        0: [(w_kv[None], 0), (w_q.reshape(1, n_b * d, d), 0)],
        1: [(w_o.reshape(1, n_b * d, d), 0)],
    }
    w_kv16 = w_q16 = w_o16 = k = vT = None
    for l in range(DEPTH):
        if l < N_A_LAYERS:
            h = _mixer(h, mix_norm[l], pool16, l, pool_scale[l])
        else:
            j = l - N_A_LAYERS
            lam_init = 0.8 - 0.6 * math.exp(-0.3 * l)
            qT, = _proj(h, mix_norm[l], w_q16, j, q_tabs, ["rope_t"])
            o = _diff_attention(qT, k, vT, lam[j], subln[j], lam_init=lam_init)
            h = _oproj(o, w_o16, j, h)
        side = [(w_mlp_in, l + 1), (w_mlp_out, l + 1)] if l + 1 < DEPTH else []
        side += side_by_layer.get(l, [])
        h, cast = _mlp(h, mlp_norm[l], w_in16, w_out16, final_norm, side, final=(l == DEPTH - 1))
        if l + 1 < DEPTH:
            w_in16, w_out16 = cast[0], cast[1]
        if l == 0:
            w_kv16, w_q16 = cast[2][None], cast[3].reshape(n_b, d, d)
        if l == 1:
            w_o16 = cast[2].reshape(n_b, d, d)
        if l == N_A_LAYERS - 1:
            k, vT = _proj(h, kv_norm, w_kv16, 0, k_tabs, ["rope", "tile_t"])
    return h.reshape(b, s, d)
```

```python
import functools
import math

import jax
import jax.numpy as jnp
from jax import lax
from jax.experimental import pallas as pl
from jax.experimental.pallas import tpu as pltpu

D_MODEL = 2048
SEQ = 8192
DEPTH = 4
CHUNK = 64
N_A_LAYERS = DEPTH // 2
POOL_WINDOWS = (2, 4, 8, 16)
POOL_GROUP = D_MODEL // len(POOL_WINDOWS)
N_HEADS = 8
HEAD_DIM = D_MODEL // (2 * N_HEADS)
V_HEAD_DIM = 2 * HEAD_DIM
ROT_DIM = HEAD_DIM // 4
ROPE_THETA = 500000.0
D_FF = 4 * D_MODEL
EPS = 1e-6

LANES = 128
POOL_HALO = 16
MASK_VALUE = -1e30
VMEM_LIMIT = 56 * 1024 * 1024

F32 = jnp.float32
BF16 = jnp.bfloat16


def _rms(x, g):
    return x * lax.rsqrt(jnp.mean(x * x, axis=-1, keepdims=True) + EPS) * g


def _params(*sem):
    return pltpu.CompilerParams(dimension_semantics=sem, vmem_limit_bytes=VMEM_LIMIT)


def _mixer_kernel(h_ref, halo_ref, g_ref, w_ref, sc_ref, o_ref, ext_ref, *, tm):
    i = pl.program_id(0)
    g = g_ref[...]
    x = h_ref[...]
    n = _rms(x, g)
    nh = _rms(halo_ref[...], g)
    ext_ref[0:POOL_HALO, :] = jnp.where(i > 0, nh, 0.0)
    ext_ref[POOL_HALO:, :] = n
    row = i * tm + lax.broadcasted_iota(jnp.int32, (tm, 1), 0)
    for gi, w in enumerate(POOL_WINDOWS):
        cols = slice(gi * POOL_GROUP, (gi + 1) * POOL_GROUP)
        ng = n[:, cols]
        win = ext_ref[:, cols]
        sh = 1
        while sh < w:
            win = win + pltpu.roll(win, sh, 0)
            sh *= 2
        acc = win[POOL_HALO:, :]
        inv_cnt = 1.0 / jnp.minimum(row + 1, w).astype(F32)
        pooled = acc * inv_cnt - ng
        y = jnp.dot(pooled.astype(BF16), w_ref[gi], preferred_element_type=F32)
        o_ref[:, cols] = x[:, cols] + y * sc_ref[:, cols]


def _mixer(h, g, w_pool, layer, scale, *, tm=1024):
    s, d = h.shape
    return pl.pallas_call(
        functools.partial(_mixer_kernel, tm=tm),
        grid=(s // tm,),
        in_specs=[
            pl.BlockSpec((tm, d), lambda i: (i, 0)),
            pl.BlockSpec((POOL_HALO, d), lambda i: (jnp.maximum(i * (tm // POOL_HALO) - 1, 0), 0)),
            pl.BlockSpec((1, d), lambda i: (0, 0)),
            pl.BlockSpec((None,) + w_pool.shape[1:], lambda i: (layer, 0, 0, 0)),
            pl.BlockSpec((1, d), lambda i: (0, 0)),
        ],
        out_specs=pl.BlockSpec((tm, d), lambda i: (i, 0)),
        out_shape=jax.ShapeDtypeStruct((s, d), F32),
        scratch_shapes=[pltpu.VMEM((tm + POOL_HALO, d), F32)],
        compiler_params=_params("arbitrary"),
        name="pool_mixer",
    )(h, h, g.reshape(1, d), w_pool, scale.reshape(1, d))


MLP_PARTS = 2


def _mlp_kernel(*refs, final, n_side):
    x_ref, g_ref, win_ref, wout_ref, fg_ref = refs[:5]
    side_in = refs[5:5 + n_side]
    o_ref = refs[5 + n_side]
    side_out = refs[6 + n_side:6 + 2 * n_side]
    xn_ref = refs[6 + 2 * n_side]
    k = pl.program_id(1)

    @pl.when(k == 0)
    def _():
        x = x_ref[...]
        xn_ref[...] = _rms(x, g_ref[...]).astype(BF16)
        o_ref[...] = x

    xn = xn_ref[...]
    part = win_ref.shape[1] // MLP_PARTS
    us = []
    for p in range(MLP_PARTS):
        u = jnp.dot(xn, win_ref[:, p * part:(p + 1) * part], preferred_element_type=F32)
        u = jnp.maximum(u, 0.0)
        us.append((u * u).astype(BF16))
    o_ref[...] += jnp.dot(jnp.concatenate(us, axis=1), wout_ref[...], preferred_element_type=F32)

    for src, dst in zip(side_in, side_out):
        dst[...] = src[...].astype(BF16)

    if final:
        @pl.when(k == pl.num_programs(1) - 1)
        def _():
            o_ref[...] = _rms(o_ref[...], fg_ref[...])


def _mlp(h, g, w_in, w_out, final_g, side, *, final, tm=512, tf=1024):
    s, d = h.shape
    f = w_in.shape[1]
    nk = f // tf
    steps = (s // tm) * nk
    side_specs, side_out_specs, side_shapes = [], [], []
    for arr, layer in side:
        _, rows, cols = arr.shape
        rps = rows // steps
        side_specs.append(pl.BlockSpec((None, rps, cols), lambda i, k, layer=layer: (layer, i * nk + k, 0)))
        side_out_specs.append(pl.BlockSpec((rps, cols), lambda i, k: (i * nk + k, 0)))
        side_shapes.append(jax.ShapeDtypeStruct((rows, cols), BF16))
    outs = pl.pallas_call(
        functools.partial(_mlp_kernel, final=final, n_side=len(side)),
        grid=(s // tm, nk),
        in_specs=[
            pl.BlockSpec((tm, d), lambda i, k: (i, 0)),
            pl.BlockSpec((1, d), lambda i, k: (0, 0)),
            pl.BlockSpec((d, tf), lambda i, k: (0, k)),
            pl.BlockSpec((tf, d), lambda i, k: (k, 0)),
            pl.BlockSpec((1, d), lambda i, k: (0, 0)),
        ] + side_specs,
        out_specs=[pl.BlockSpec((tm, d), lambda i, k: (i, 0))] + side_out_specs,
        out_shape=[jax.ShapeDtypeStruct((s, d), F32)] + side_shapes,
        scratch_shapes=[pltpu.VMEM((tm, d), BF16)],
        compiler_params=_params("arbitrary", "arbitrary"),
        name="sqrelu_mlp",
    )(h, g.reshape(1, d), w_in, w_out, final_g.reshape(1, d), *[arr for arr, _ in side])
    return outs[0], outs[1:]


def _proj_kernel(x_ref, g_ref, w_ref, c_ref, a_ref, b_ref, *out_refs, modes):
    xn = _rms(x_ref[...], g_ref[...]).astype(BF16)
    n = D_MODEL
    for jj, (mode, o_ref) in enumerate(zip(modes, out_refs)):
        y = jnp.dot(xn, w_ref[:, jj * n:(jj + 1) * n], preferred_element_type=F32)
        for s in range(n // LANES):
            slab = slice(s * LANES, (s + 1) * LANES)
            t = y[:, slab]
            if mode != "tile_t":
                t = (t * c_ref[...] + pltpu.roll(t, LANES - ROT_DIM // 2, 1) * a_ref[...]
                     + pltpu.roll(t, ROT_DIM // 2, 1) * b_ref[...])
            if mode == "rope":
                o_ref[:, slab] = t.astype(BF16)
            elif mode == "rope_t":
                o_ref[slab, :] = t.T.astype(BF16)
            else:
                tt = t.T.astype(BF16)
                for r in range(o_ref.shape[0]):
                    o_ref[r, slab, :] = tt[:, r * ATTN_KEY_TILE:(r + 1) * ATTN_KEY_TILE]


def _proj(h, g, w, layer, tabs, modes, *, tm=512):
    s, d = h.shape
    n = D_MODEL
    out_specs, out_shapes = [], []
    for mode in modes:
        if mode == "rope":
            out_specs.append(pl.BlockSpec((tm, n), lambda i: (i, 0)))
            out_shapes.append(jax.ShapeDtypeStruct((s, n), BF16))
        elif mode == "rope_t":
            out_specs.append(pl.BlockSpec((n, tm), lambda i: (0, i)))
            out_shapes.append(jax.ShapeDtypeStruct((n, s), BF16))
        else:
            out_specs.append(pl.BlockSpec((tm // ATTN_KEY_TILE, n, ATTN_KEY_TILE), lambda i: (i, 0, 0)))
            out_shapes.append(jax.ShapeDtypeStruct((s // ATTN_KEY_TILE, n, ATTN_KEY_TILE), BF16))
    tab_spec = pl.BlockSpec((tm, LANES), lambda i: (i, 0))
    return pl.pallas_call(
        functools.partial(_proj_kernel, modes=tuple(modes)),
        grid=(s // tm,),
        in_specs=[
            pl.BlockSpec((tm, d), lambda i: (i, 0)),
            pl.BlockSpec((1, d), lambda i: (0, 0)),
            pl.BlockSpec((None, d, n * len(modes)), lambda i: (layer, 0, 0), pipeline_mode=pl.Buffered(1)),
            tab_spec, tab_spec, tab_spec,
        ],
        out_specs=out_specs,
        out_shape=out_shapes,
        compiler_params=_params("arbitrary"),
        name="norm_proj",
    )(h, g.reshape(1, d), w, *tabs)


def _rope_tables(seq, scale):
    half = ROT_DIM // 2
    pos = jnp.arange(seq, dtype=F32)
    inv_freq = ROPE_THETA ** (-jnp.arange(0, ROT_DIM, 2, dtype=F32) / ROT_DIM)
    ang = pos[:, None] * inv_freq[None, :]
    cos, sin = jnp.cos(ang), jnp.sin(ang)
    zeros = jnp.zeros((seq, half), F32)
    keep = LANES - ROT_DIM
    c = jnp.concatenate([cos, cos, jnp.ones((seq, keep), F32)], axis=1)
    a = jnp.concatenate([-sin, zeros, jnp.zeros((seq, keep), F32)], axis=1)
    b = jnp.concatenate([zeros, sin, jnp.zeros((seq, keep), F32)], axis=1)
    return c * scale, a * scale, b * scale


ATTN_Q_TILE = 2048
ATTN_KEY_BLOCK = 1024
ATTN_Q_SUB = 512
ATTN_KEY_TILE = 512


def _attn_kernel(qT_ref, k_ref, vT_ref, lam_ref, sg_ref, o_ref, m_ref, l_ref, acc_ref, s_ref, *, lam_init):
    t, kblock, sub, kt = ATTN_Q_TILE, ATTN_KEY_BLOCK, ATTN_Q_SUB, ATTN_KEY_TILE
    qi = pl.program_id(1)
    n_full = qi * (t // kblock)

    def init():
        m_ref[...] = jnp.full(m_ref.shape, MASK_VALUE, F32)
        l_ref[...] = jnp.zeros(l_ref.shape, F32)
        acc_ref[...] = jnp.zeros(acc_ref.shape, F32)

    def full_chains(kb):
        return [(c, q0, kblock // kt, kb * (kblock // kt), False) for q0 in range(0, t, sub) for c in range(2)]

    def diag_chains():
        return [(c, q0, (q0 + sub) // kt, qi * (t // kt), True) for q0 in range(0, t, sub) for c in range(2)]

    def score_tile(chain, j, slot):
        c, q0, _, tile0, masked = chain
        hd = slice(c * HEAD_DIM, (c + 1) * HEAD_DIM)
        k = k_ref[pl.ds(pl.multiple_of((tile0 + j) * kt, kt), kt), hd]
        s = jnp.dot(k, qT_ref[hd, q0:q0 + sub], preferred_element_type=F32)
        if masked and (j + 1) * kt > q0:
            kc = (j * kt + lax.broadcasted_iota(jnp.int32, (kt, sub), 0)) // CHUNK
            qc = (q0 + lax.broadcasted_iota(jnp.int32, (kt, sub), 1)) // CHUNK
            s = jnp.where(kc <= qc, s, MASK_VALUE)
        s_ref[slot, j * kt:(j + 1) * kt, :] = s
        return jnp.max(s, axis=0, keepdims=True)

    def value_tile(chain, j, slot, m_new):
        p = jnp.exp2(s_ref[slot, j * kt:(j + 1) * kt, :] - m_new)
        pv = jnp.dot(vT_ref[chain[3] + j], p.astype(BF16), preferred_element_type=F32)
        return jnp.sum(p, axis=0, keepdims=True), pv

    def sweep(chains, maxes, lead, trailing=None):
        seq = chains + ([trailing] if trailing is not None else [])
        for i, chain in enumerate(chains):
            c, q0, n, _, _ = chain
            qs = slice(q0, q0 + sub)
            m_old = m_ref[c, :, qs]
            m_new = functools.reduce(jnp.maximum, maxes, m_old)
            alpha = jnp.exp2(m_old - m_new)
            nxt = seq[i + 1] if i + 1 < len(seq) else None
            n_nxt = nxt[2] if nxt is not None else 0
            maxes, l_new, acc = [], alpha * l_ref[c, :, qs], None
            for j in range(n + 1):
                lead_end = n_nxt if j == n else min(n_nxt, j + lead + 1)
                for js in range(len(maxes), lead_end):
                    maxes.append(score_tile(nxt, js, (i + 1) % 2))
                if j < n:
                    l_j, pv = value_tile(chain, j, i % 2, m_new)
                    l_new = l_new + l_j
                    acc = alpha * acc_ref[c, :, qs] + pv if acc is None else acc + pv
            l_ref[c, :, qs] = l_new
            acc_ref[c, :, qs] = acc
            m_ref[c, :, qs] = m_new
        return maxes

    def first_scores(chain):
        return [score_tile(chain, j, 0) for j in range(chain[2])]

    def finalize():
        lam = lam_ref[...]
        lam_full = (jnp.exp(jnp.sum(lam[0:1] * lam[1:2], axis=1, keepdims=True))
                    - jnp.exp(jnp.sum(lam[2:3] * lam[3:4], axis=1, keepdims=True)) + lam_init)
        oT = acc_ref[0] * (1.0 / l_ref[0]) - acc_ref[1] * (lam_full / l_ref[1])
        inv_rms = lax.rsqrt(jnp.mean(oT * oT, axis=0, keepdims=True) + EPS) * (1.0 - lam_init)
        o_ref[...] = (oT * inv_rms * sg_ref[...]).astype(BF16)

    @pl.when(qi == 0)
    def _():
        init()
        chains = diag_chains()
        sweep(chains, first_scores(chains[0]), 1)
        finalize()

    @pl.when(qi > 0)
    def _():
        init()

        def body(kb, maxes):
            return tuple(sweep(full_chains(kb), list(maxes), 0, trailing=full_chains(kb + 1)[0]))

        maxes = lax.fori_loop(0, n_full - 1, body, tuple(first_scores(full_chains(0)[0])))
        sweep(full_chains(n_full - 1) + diag_chains(), list(maxes), 1)
        finalize()


def _diff_attention(qT, k, vT, lam, subln_g, *, lam_init):
    s = k.shape[0]
    t, sub, kt = ATTN_Q_TILE, ATTN_Q_SUB, ATTN_KEY_TILE
    return pl.pallas_call(
        functools.partial(_attn_kernel, lam_init=lam_init),
        grid=(N_HEADS, s // t),
        in_specs=[
            pl.BlockSpec((V_HEAD_DIM, t), lambda h, i: (h, i)),
            pl.BlockSpec((s, V_HEAD_DIM), lambda h, i: (0, h)),
            pl.BlockSpec((s // kt, V_HEAD_DIM, kt), lambda h, i: (0, h, 0)),
            pl.BlockSpec((4, HEAD_DIM), lambda h, i: (0, 0)),
            pl.BlockSpec((V_HEAD_DIM, 1), lambda h, i: (0, 0)),
        ],
        out_specs=pl.BlockSpec((V_HEAD_DIM, t), lambda h, i: (h, i)),
        out_shape=jax.ShapeDtypeStruct((D_MODEL, s), BF16),
        scratch_shapes=[
            pltpu.VMEM((2, 1, t), F32),
            pltpu.VMEM((2, 1, t), F32),
            pltpu.VMEM((2, V_HEAD_DIM, t), F32),
            pltpu.VMEM((2, t, sub), F32),
        ],
        compiler_params=_params("arbitrary", "arbitrary"),
        name="diff_attention",
    )(qT, k, vT, lam, subln_g.reshape(V_HEAD_DIM, 1))


def _oproj_kernel(oT_ref, w_ref, h_ref, out_ref):
    y = lax.dot_general(oT_ref[...], w_ref[...], (((0,), (0,)), ((), ())), preferred_element_type=F32)
    out_ref[...] = h_ref[...] + y


def _oproj(o, w, layer, h, *, tm=512, tn=2048):
    s, d = h.shape
    return pl.pallas_call(
        _oproj_kernel,
        grid=(s // tm, d // tn),
        in_specs=[
            pl.BlockSpec((d, tm), lambda i, j: (0, i)),
            pl.BlockSpec((None, d, tn), lambda i, j: (layer, 0, j)),
            pl.BlockSpec((tm, tn), lambda i, j: (i, j)),
        ],
        out_specs=pl.BlockSpec((tm, tn), lambda i, j: (i, j)),
        out_shape=jax.ShapeDtypeStruct((s, d), F32),
        compiler_params=_params("arbitrary", "arbitrary"),
        name="attn_out_proj",
    )(o, w, h)


CAST_BLOCK_ELEMS = 2 * 1024 * 1024


def _cast_kernel(w_ref, o_ref):
    o_ref[...] = w_ref[...].astype(BF16)


def _to_bf16(w, layer):
    _, rows, cols = w.shape
    rb = min(rows, CAST_BLOCK_ELEMS // cols)
    return pl.pallas_call(
        _cast_kernel,
        grid=(rows // rb,),
        in_specs=[pl.BlockSpec((None, rb, cols), lambda i: (layer, i, 0))],
        out_specs=pl.BlockSpec((rb, cols), lambda i: (i, 0)),
        out_shape=jax.ShapeDtypeStruct((rows, cols), BF16),
        compiler_params=_params("arbitrary"),
        name="cast_bf16",
    )(w)


def kernel(x, mix_norm, mlp_norm, pool_w, pool_scale, kv_norm, w_kv, w_q, lam, subln, w_o,
           w_mlp_in, w_mlp_out, final_norm):
    b, s, d = x.shape
    assert (b, s, d) == (1, SEQ, D_MODEL)
    h = x.reshape(s, d)
    k_tabs = _rope_tables(s, 1.0)
    q_tabs = _rope_tables(s, HEAD_DIM ** -0.5 * math.log2(math.e))
    n_b = DEPTH - N_A_LAYERS
    pool16 = _to_bf16(pool_w.reshape(1, -1, POOL_GROUP), 0).reshape(pool_w.shape)
    w_in16, w_out16 = _to_bf16(w_mlp_in, 0), _to_bf16(w_mlp_out, 0)
    side_by_layer = {
        0: [(w_kv[None], 0), (w_q.reshape(1, n_b * d, d), 0)],
        1: [(w_o.reshape(1, n_b * d, d), 0)],
    }
    w_kv16 = w_q16 = w_o16 = k = vT = None
    for l in range(DEPTH):
        if l < N_A_LAYERS:
            h = _mixer(h, mix_norm[l], pool16, l, pool_scale[l])
        else:
            j = l - N_A_LAYERS
            lam_init = 0.8 - 0.6 * math.exp(-0.3 * l)
            qT, = _proj(h, mix_norm[l], w_q16, j, q_tabs, ["rope_t"])
            o = _diff_attention(qT, k, vT, lam[j], subln[j], lam_init=lam_init)
            h = _oproj(o, w_o16, j, h)
        side = [(w_mlp_in, l + 1), (w_mlp_out, l + 1)] if l + 1 < DEPTH else []
        side += side_by_layer.get(l, [])
        h, cast = _mlp(h, mlp_norm[l], w_in16, w_out16, final_norm, side, final=(l == DEPTH - 1))
        if l + 1 < DEPTH:
            w_in16, w_out16 = cast[0], cast[1]
        if l == 0:
            w_kv16, w_q16 = cast[2][None], cast[3].reshape(n_b, d, d)
        if l == 1:
            w_o16 = cast[2].reshape(n_b, d, d)
        if l == N_A_LAYERS - 1:
            k, vT = _proj(h, kv_norm, w_kv16, 0, k_tabs, ["rope", "tile_t"])
    return h.reshape(b, s, d)
```

```python
import functools
import math

import jax
import jax.numpy as jnp
from jax import lax
from jax.experimental import pallas as pl
from jax.experimental.pallas import tpu as pltpu

D_MODEL = 2048
SEQ = 8192
DEPTH = 4
CHUNK = 64
N_A_LAYERS = DEPTH // 2
POOL_WINDOWS = (2, 4, 8, 16)
POOL_GROUP = D_MODEL // len(POOL_WINDOWS)
N_HEADS = 8
HEAD_DIM = D_MODEL // (2 * N_HEADS)
V_HEAD_DIM = 2 * HEAD_DIM
ROT_DIM = HEAD_DIM // 4
ROPE_THETA = 500000.0
D_FF = 4 * D_MODEL
EPS = 1e-6

LANES = 128
POOL_HALO = 16
MASK_VALUE = float(jnp.finfo(jnp.float32).min)
VMEM_LIMIT = 56 * 1024 * 1024

F32 = jnp.float32
BF16 = jnp.bfloat16


def _rms(x, g):
    return x * lax.rsqrt(jnp.mean(x * x, axis=-1, keepdims=True) + EPS) * g


def _params(*sem):
    return pltpu.CompilerParams(dimension_semantics=sem, vmem_limit_bytes=VMEM_LIMIT)


def _mixer_kernel(h_ref, halo_ref, g_ref, w_ref, sc_ref, o_ref, ext_ref, *, tm):
    i = pl.program_id(0)
    g = g_ref[...]
    x = h_ref[...]
    n = _rms(x, g)
    nh = _rms(halo_ref[...], g)
    ext_ref[0:POOL_HALO, :] = jnp.where(i > 0, nh, 0.0)
    ext_ref[POOL_HALO:, :] = n
    row = i * tm + lax.broadcasted_iota(jnp.int32, (tm, 1), 0)
    for gi, w in enumerate(POOL_WINDOWS):
        cols = slice(gi * POOL_GROUP, (gi + 1) * POOL_GROUP)
        ng = n[:, cols]
        win = ext_ref[:, cols]
        sh = 1
        while sh < w:
            win = win + pltpu.roll(win, sh, 0)
            sh *= 2
        acc = win[POOL_HALO:, :]
        inv_cnt = 1.0 / jnp.minimum(row + 1, w).astype(F32)
        pooled = acc * inv_cnt - ng
        y = jnp.dot(pooled.astype(BF16), w_ref[gi], preferred_element_type=F32)
        o_ref[:, cols] = x[:, cols] + y * sc_ref[:, cols]


def _mixer(h, g, w_pool, layer, scale, *, tm=1024):
    s, d = h.shape
    return pl.pallas_call(
        functools.partial(_mixer_kernel, tm=tm),
        grid=(s // tm,),
        in_specs=[
            pl.BlockSpec((tm, d), lambda i: (i, 0)),
            pl.BlockSpec((POOL_HALO, d), lambda i: (jnp.maximum(i * (tm // POOL_HALO) - 1, 0), 0)),
            pl.BlockSpec((1, d), lambda i: (0, 0)),
            pl.BlockSpec((None,) + w_pool.shape[1:], lambda i: (layer, 0, 0, 0)),
            pl.BlockSpec((1, d), lambda i: (0, 0)),
        ],
        out_specs=pl.BlockSpec((tm, d), lambda i: (i, 0)),
        out_shape=jax.ShapeDtypeStruct((s, d), F32),
        scratch_shapes=[pltpu.VMEM((tm + POOL_HALO, d), F32)],
        compiler_params=_params("arbitrary"),
        name="pool_mixer",
    )(h, h, g.reshape(1, d), w_pool, scale.reshape(1, d))


MLP_PARTS = 2


def _mlp_kernel(*refs, final, n_side, nk):
    x_ref, xnext_ref, g_ref, win_ref, wout_ref, fg_ref = refs[:6]
    side_in = refs[6:6 + n_side]
    o_ref = refs[6 + n_side]
    side_out = refs[7 + n_side:7 + 2 * n_side]
    xn_ref = refs[7 + 2 * n_side]
    i = pl.program_id(0)
    k = pl.program_id(1)
    slot = i % 2

    @pl.when(k == 0)
    def _():
        o_ref[...] = x_ref[...]

    @pl.when((k == 0) & (i == 0))
    def _():
        xn_ref[0] = _rms(x_ref[...], g_ref[...]).astype(BF16)

    xn = xn_ref[slot]
    part = win_ref.shape[1] // MLP_PARTS
    us = []
    for p in range(MLP_PARTS):
        u = jnp.dot(xn, win_ref[:, p * part:(p + 1) * part], preferred_element_type=F32)
        u = jnp.maximum(u, 0.0)
        us.append((u * u).astype(BF16))
        if p == 0:
            rows = x_ref.shape[0] // nk
            r0 = pl.multiple_of(k * rows, rows)
            xn_ref[1 - slot, pl.ds(r0, rows), :] = _rms(xnext_ref[pl.ds(r0, rows), :], g_ref[...]).astype(BF16)
    o_ref[...] += jnp.dot(jnp.concatenate(us, axis=1), wout_ref[...], preferred_element_type=F32)

    for src, dst in zip(side_in, side_out):
        dst[...] = src[...].astype(BF16)

    if final:
        @pl.when(k == pl.num_programs(1) - 1)
        def _():
            o_ref[...] = _rms(o_ref[...], fg_ref[...])


def _mlp(h, g, w_in, w_out, final_g, side, *, final, tm=512, tf=1024):
    s, d = h.shape
    f = w_in.shape[1]
    nk = f // tf
    steps = (s // tm) * nk
    side_specs, side_out_specs, side_shapes = [], [], []
    for arr, layer in side:
        _, rows, cols = arr.shape
        rps = rows // steps
        side_specs.append(pl.BlockSpec((None, rps, cols), lambda i, k, layer=layer: (layer, i * nk + k, 0)))
        side_out_specs.append(pl.BlockSpec((rps, cols), lambda i, k: (i * nk + k, 0)))
        side_shapes.append(jax.ShapeDtypeStruct((rows, cols), BF16))
    outs = pl.pallas_call(
        functools.partial(_mlp_kernel, final=final, n_side=len(side), nk=nk),
        grid=(s // tm, nk),
        in_specs=[
            pl.BlockSpec((tm, d), lambda i, k: (i, 0)),
            pl.BlockSpec((tm, d), lambda i, k: (jnp.minimum(i + 1, s // tm - 1), 0)),
            pl.BlockSpec((1, d), lambda i, k: (0, 0)),
            pl.BlockSpec((d, tf), lambda i, k: (0, k)),
            pl.BlockSpec((tf, d), lambda i, k: (k, 0)),
            pl.BlockSpec((1, d), lambda i, k: (0, 0)),
        ] + side_specs,
        out_specs=[pl.BlockSpec((tm, d), lambda i, k: (i, 0))] + side_out_specs,
        out_shape=[jax.ShapeDtypeStruct((s, d), F32)] + side_shapes,
        scratch_shapes=[pltpu.VMEM((2, tm, d), BF16)],
        compiler_params=_params("arbitrary", "arbitrary"),
        name="sqrelu_mlp",
    )(h, h, g.reshape(1, d), w_in, w_out, final_g.reshape(1, d), *[arr for arr, _ in side])
    return outs[0], outs[1:]


def _proj_kernel(x_ref, g_ref, w_ref, c_ref, a_ref, b_ref, *out_refs, modes):
    xn = _rms(x_ref[...], g_ref[...]).astype(BF16)
    n = D_MODEL
    for jj, (mode, o_ref) in enumerate(zip(modes, out_refs)):
        y = jnp.dot(xn, w_ref[:, jj * n:(jj + 1) * n], preferred_element_type=F32)
        for s in range(n // LANES):
            slab = slice(s * LANES, (s + 1) * LANES)
            t = y[:, slab]
            if mode != "tile_t":
                t = (t * c_ref[...] + pltpu.roll(t, LANES - ROT_DIM // 2, 1) * a_ref[...]
                     + pltpu.roll(t, ROT_DIM // 2, 1) * b_ref[...])
            if mode == "rope":
                o_ref[:, slab] = t.astype(BF16)
            elif mode == "rope_t":
                o_ref[slab, :] = t.T.astype(BF16)
            else:
                tt = t.T.astype(BF16)
                for r in range(o_ref.shape[0]):
                    o_ref[r, slab, :] = tt[:, r * ATTN_KEY_TILE:(r + 1) * ATTN_KEY_TILE]


def _proj(h, g, w, layer, tabs, modes, *, tm=512):
    s, d = h.shape
    n = D_MODEL
    out_specs, out_shapes = [], []
    for mode in modes:
        if mode == "rope":
            out_specs.append(pl.BlockSpec((tm, n), lambda i: (i, 0)))
            out_shapes.append(jax.ShapeDtypeStruct((s, n), BF16))
        elif mode == "rope_t":
            out_specs.append(pl.BlockSpec((n, tm), lambda i: (0, i)))
            out_shapes.append(jax.ShapeDtypeStruct((n, s), BF16))
        else:
            out_specs.append(pl.BlockSpec((tm // ATTN_KEY_TILE, n, ATTN_KEY_TILE), lambda i: (i, 0, 0)))
            out_shapes.append(jax.ShapeDtypeStruct((s // ATTN_KEY_TILE, n, ATTN_KEY_TILE), BF16))
    tab_spec = pl.BlockSpec((tm, LANES), lambda i: (i, 0))
    return pl.pallas_call(
        functools.partial(_proj_kernel, modes=tuple(modes)),
        grid=(s // tm,),
        in_specs=[
            pl.BlockSpec((tm, d), lambda i: (i, 0)),
            pl.BlockSpec((1, d), lambda i: (0, 0)),
            pl.BlockSpec((None, d, n * len(modes)), lambda i: (layer, 0, 0), pipeline_mode=pl.Buffered(1)),
            tab_spec, tab_spec, tab_spec,
        ],
        out_specs=out_specs,
        out_shape=out_shapes,
        compiler_params=_params("arbitrary"),
        name="norm_proj",
    )(h, g.reshape(1, d), w, *tabs)


def _rope_tables(seq, scale):
    half = ROT_DIM // 2
    pos = jnp.arange(seq, dtype=F32)
    inv_freq = ROPE_THETA ** (-jnp.arange(0, ROT_DIM, 2, dtype=F32) / ROT_DIM)
    ang = pos[:, None] * inv_freq[None, :]
    cos, sin = jnp.cos(ang), jnp.sin(ang)
    zeros = jnp.zeros((seq, half), F32)
    keep = LANES - ROT_DIM
    c = jnp.concatenate([cos, cos, jnp.ones((seq, keep), F32)], axis=1)
    a = jnp.concatenate([-sin, zeros, jnp.zeros((seq, keep), F32)], axis=1)
    b = jnp.concatenate([zeros, sin, jnp.zeros((seq, keep), F32)], axis=1)
    return c * scale, a * scale, b * scale


ATTN_Q_TILE = 2048
ATTN_KEY_BLOCK = 1024
ATTN_Q_SUB = 512
ATTN_KEY_TILE = 512


def _attn_kernel(qT_ref, k_ref, vT_ref, lam_ref, sg_ref, o_ref, m_ref, l_ref, acc_ref, s_ref, *, lam_init):
    t, kblock, sub, kt = ATTN_Q_TILE, ATTN_KEY_BLOCK, ATTN_Q_SUB, ATTN_KEY_TILE
    qi = pl.program_id(1)
    n_full = qi * (t // kblock)

    def init():
        m_ref[...] = jnp.full(m_ref.shape, MASK_VALUE, F32)
        l_ref[...] = jnp.zeros(l_ref.shape, F32)
        acc_ref[...] = jnp.zeros(acc_ref.shape, F32)

    def full_chains(kb):
        return [(c, q0, kblock // kt, kb * (kblock // kt), False) for q0 in range(0, t, sub) for c in range(2)]

    def diag_chains():
        return [(c, q0, (q0 + sub) // kt, qi * (t // kt), True) for q0 in range(0, t, sub) for c in range(2)]

    def score_tile(chain, j, slot):
        c, q0, _, tile0, masked = chain
        hd = slice(c * HEAD_DIM, (c + 1) * HEAD_DIM)
        k = k_ref[pl.ds(pl.multiple_of((tile0 + j) * kt, kt), kt), hd]
        s = jnp.dot(k, qT_ref[hd, q0:q0 + sub], preferred_element_type=F32)
        if masked and (j + 1) * kt > q0:
            kc = (j * kt + lax.broadcasted_iota(jnp.int32, (kt, sub), 0)) // CHUNK
            qc = (q0 + lax.broadcasted_iota(jnp.int32, (kt, sub), 1)) // CHUNK
            s = jnp.where(kc <= qc, s, MASK_VALUE)
        s_ref[slot, j * kt:(j + 1) * kt, :] = s
        return jnp.max(s, axis=0, keepdims=True)

    def value_tile(chain, j, slot, m_new):
        p = jnp.exp2(s_ref[slot, j * kt:(j + 1) * kt, :] - m_new)
        pv = jnp.dot(vT_ref[chain[3] + j], p.astype(BF16), preferred_element_type=F32)
        return jnp.sum(p, axis=0, keepdims=True), pv

    def sweep(chains, maxes, trailing=None):
        seq = chains + ([trailing] if trailing is not None else [])
        for i, chain in enumerate(chains):
            c, q0, n, _, _ = chain
            qs = slice(q0, q0 + sub)
            m_old = m_ref[c, :, qs]
            m_new = functools.reduce(jnp.maximum, maxes, m_old)
            alpha = jnp.exp2(m_old - m_new)
            nxt = seq[i + 1] if i + 1 < len(seq) else None
            n_nxt = nxt[2] if nxt is not None else 0
            maxes, l_new, acc = [], alpha * l_ref[c, :, qs], None
            lead = 0 if nxt is not None and not chain[4] and not nxt[4] else 1
            for j in range(n + 1):
                lead_end = n_nxt if j == n else min(n_nxt, j + lead + 1)
                for js in range(len(maxes), lead_end):
                    maxes.append(score_tile(nxt, js, (i + 1) % 2))
                if j < n:
                    l_j, pv = value_tile(chain, j, i % 2, m_new)
                    l_new = l_new + l_j
                    acc = alpha * acc_ref[c, :, qs] + pv if acc is None else acc + pv
            l_ref[c, :, qs] = l_new
            acc_ref[c, :, qs] = acc
            m_ref[c, :, qs] = m_new
        return maxes

    def first_scores(chain):
        return [score_tile(chain, j, 0) for j in range(chain[2])]

    def finalize():
        lam = lam_ref[...]
        lam_full = (jnp.exp(jnp.sum(lam[0:1] * lam[1:2], axis=1, keepdims=True))
                    - jnp.exp(jnp.sum(lam[2:3] * lam[3:4], axis=1, keepdims=True)) + lam_init)
        oT = acc_ref[0] * (1.0 / l_ref[0]) - acc_ref[1] * (lam_full / l_ref[1])
        inv_rms = lax.rsqrt(jnp.mean(oT * oT, axis=0, keepdims=True) + EPS) * (1.0 - lam_init)
        o_ref[...] = (oT * inv_rms * sg_ref[...]).astype(BF16)

    @pl.when(qi == 0)
    def _():
        init()
        chains = diag_chains()
        sweep(chains, first_scores(chains[0]))
        finalize()

    @pl.when(qi > 0)
    def _():
        init()

        def body(kb, maxes):
            return tuple(sweep(full_chains(kb), list(maxes), trailing=full_chains(kb + 1)[0]))

        maxes = lax.fori_loop(0, n_full - 1, body, tuple(first_scores(full_chains(0)[0])))
        sweep(full_chains(n_full - 1) + diag_chains(), list(maxes))
        finalize()


def _diff_attention(qT, k, vT, lam, subln_g, *, lam_init):
    s = k.shape[0]
    t, sub, kt = ATTN_Q_TILE, ATTN_Q_SUB, ATTN_KEY_TILE
    return pl.pallas_call(
        functools.partial(_attn_kernel, lam_init=lam_init),
        grid=(N_HEADS, s // t),
        in_specs=[
            pl.BlockSpec((V_HEAD_DIM, t), lambda h, i: (h, i)),
            pl.BlockSpec((s, V_HEAD_DIM), lambda h, i: (0, h)),
            pl.BlockSpec((s // kt, V_HEAD_DIM, kt), lambda h, i: (0, h, 0)),
            pl.BlockSpec((4, HEAD_DIM), lambda h, i: (0, 0)),
            pl.BlockSpec((V_HEAD_DIM, 1), lambda h, i: (0, 0)),
        ],
        out_specs=pl.BlockSpec((V_HEAD_DIM, t), lambda h, i: (h, i)),
        out_shape=jax.ShapeDtypeStruct((D_MODEL, s), BF16),
        scratch_shapes=[
            pltpu.VMEM((2, 1, t), F32),
            pltpu.VMEM((2, 1, t), F32),
            pltpu.VMEM((2, V_HEAD_DIM, t), F32),
            pltpu.VMEM((2, t, sub), F32),
        ],
        compiler_params=_params("arbitrary", "arbitrary"),
        name="diff_attention",
    )(qT, k, vT, lam, subln_g.reshape(V_HEAD_DIM, 1))


def _oproj_kernel(oT_ref, w_ref, h_ref, out_ref):
    y = lax.dot_general(oT_ref[...], w_ref[...], (((0,), (0,)), ((), ())), preferred_element_type=F32)
    out_ref[...] = h_ref[...] + y


def _oproj(o, w, layer, h, *, tm=512, tn=2048):
    s, d = h.shape
    return pl.pallas_call(
        _oproj_kernel,
        grid=(s // tm, d // tn),
        in_specs=[
            pl.BlockSpec((d, tm), lambda i, j: (0, i)),
            pl.BlockSpec((None, d, tn), lambda i, j: (layer, 0, j)),
            pl.BlockSpec((tm, tn), lambda i, j: (i, j)),
        ],
        out_specs=pl.BlockSpec((tm, tn), lambda i, j: (i, j)),
        out_shape=jax.ShapeDtypeStruct((s, d), F32),
        compiler_params=_params("arbitrary", "arbitrary"),
        name="attn_out_proj",
    )(o, w, h)


CAST_BLOCK_ELEMS = 2 * 1024 * 1024


def _cast_kernel(w_ref, o_ref):
    o_ref[...] = w_ref[...].astype(BF16)


def _to_bf16(w, layer):
    _, rows, cols = w.shape
    rb = min(rows, CAST_BLOCK_ELEMS // cols)
    return pl.pallas_call(
        _cast_kernel,
        grid=(rows // rb,),
        in_specs=[pl.BlockSpec((None, rb, cols), lambda i: (layer, i, 0))],
        out_specs=pl.BlockSpec((rb, cols), lambda i: (i, 0)),
        out_shape=jax.ShapeDtypeStruct((rows, cols), BF16),
        compiler_params=_params("arbitrary"),
        name="cast_bf16",
    )(w)


def kernel(x, mix_norm, mlp_norm, pool_w, pool_scale, kv_norm, w_kv, w_q, lam, subln, w_o,
           w_mlp_in, w_mlp_out, final_norm):
    b, s, d = x.shape
    assert (b, s, d) == (1, SEQ, D_MODEL)
    h = x.reshape(s, d)
    k_tabs = _rope_tables(s, 1.0)
    q_tabs = _rope_tables(s, HEAD_DIM ** -0.5 * math.log2(math.e))
    n_b = DEPTH - N_A_LAYERS
    pool16 = _to_bf16(pool_w.reshape(1, -1, POOL_GROUP), 0).reshape(pool_w.shape)
    w_in16, w_out16 = _to_bf16(w_mlp_in, 0), _to_bf16(w_mlp_out, 0)
    side_by_layer = {
        0: [(w_kv[None], 0), (w_q.reshape(1, n_b * d, d), 0)],
        1: [(w_o.reshape(1, n_b * d, d), 0)],
    }
    w_kv16 = w_q16 = w_o16 = k = vT = None
    for l in range(DEPTH):
        if l < N_A_LAYERS:
            h = _mixer(h, mix_norm[l], pool16, l, pool_scale[l])
        else:
            j = l - N_A_LAYERS
            lam_init = 0.8 - 0.6 * math.exp(-0.3 * l)
            qT, = _proj(h, mix_norm[l], w_q16, j, q_tabs, ["rope_t"])
            o = _diff_attention(qT, k, vT, lam[j], subln[j], lam_init=lam_init)
            h = _oproj(o, w_o16, j, h)
        side = [(w_mlp_in, l + 1), (w_mlp_out, l + 1)] if l + 1 < DEPTH else []
        side += side_by_layer.get(l, [])
        h, cast = _mlp(h, mlp_norm[l], w_in16, w_out16, final_norm, side, final=(l == DEPTH - 1))
        if l + 1 < DEPTH:
            w_in16, w_out16 = cast[0], cast[1]
        if l == 0:
            w_kv16, w_q16 = cast[2][None], cast[3].reshape(n_b, d, d)
        if l == 1:
            w_o16 = cast[2].reshape(n_b, d, d)
        if l == N_A_LAYERS - 1:
            k, vT = _proj(h, kv_norm, w_kv16, 0, k_tabs, ["rope", "tile_t"])
    return h.reshape(b, s, d)
```

```python
import functools
import math

import jax
import jax.numpy as jnp
from jax import lax
from jax.experimental import pallas as pl
from jax.experimental.pallas import tpu as pltpu

D_MODEL = 2048
SEQ = 8192
DEPTH = 4
CHUNK = 64
N_A_LAYERS = DEPTH // 2
POOL_WINDOWS = (2, 4, 8, 16)
POOL_GROUP = D_MODEL // len(POOL_WINDOWS)
N_HEADS = 8
HEAD_DIM = D_MODEL // (2 * N_HEADS)
V_HEAD_DIM = 2 * HEAD_DIM
ROT_DIM = HEAD_DIM // 4
ROPE_THETA = 500000.0
D_FF = 4 * D_MODEL
EPS = 1e-6

LANES = 128
POOL_HALO = 16
MASK_VALUE = float(jnp.finfo(jnp.float32).min)
VMEM_LIMIT = 56 * 1024 * 1024

F32 = jnp.float32
BF16 = jnp.bfloat16


def _rms(x, g):
    return x * lax.rsqrt(jnp.mean(x * x, axis=-1, keepdims=True) + EPS) * g


def _params(*sem):
    return pltpu.CompilerParams(dimension_semantics=sem, vmem_limit_bytes=VMEM_LIMIT)


def _mixer_kernel(h_ref, halo_ref, g_ref, w_ref, sc_ref, o_ref, ext_ref, *, tm):
    i = pl.program_id(0)
    g = g_ref[...]
    x = h_ref[...]
    n = _rms(x, g)
    nh = _rms(halo_ref[...], g)
    ext_ref[0:POOL_HALO, :] = jnp.where(i > 0, nh, 0.0)
    ext_ref[POOL_HALO:, :] = n
    row = i * tm + lax.broadcasted_iota(jnp.int32, (tm, 1), 0)
    for gi, w in enumerate(POOL_WINDOWS):
        cols = slice(gi * POOL_GROUP, (gi + 1) * POOL_GROUP)
        ng = n[:, cols]
        win = ext_ref[:, cols]
        sh = 1
        while sh < w:
            win = win + pltpu.roll(win, sh, 0)
            sh *= 2
        acc = win[POOL_HALO:, :]
        inv_cnt = 1.0 / jnp.minimum(row + 1, w).astype(F32)
        pooled = acc * inv_cnt - ng
        y = jnp.dot(pooled.astype(BF16), w_ref[gi], preferred_element_type=F32)
        o_ref[:, cols] = x[:, cols] + y * sc_ref[:, cols]


def _mixer(h, g, w_pool, layer, scale, *, tm=1024):
    s, d = h.shape
    return pl.pallas_call(
        functools.partial(_mixer_kernel, tm=tm),
        grid=(s // tm,),
        in_specs=[
            pl.BlockSpec((tm, d), lambda i: (i, 0)),
            pl.BlockSpec((POOL_HALO, d), lambda i: (jnp.maximum(i * (tm // POOL_HALO) - 1, 0), 0)),
            pl.BlockSpec((1, d), lambda i: (0, 0)),
            pl.BlockSpec((None,) + w_pool.shape[1:], lambda i: (layer, 0, 0, 0)),
            pl.BlockSpec((1, d), lambda i: (0, 0)),
        ],
        out_specs=pl.BlockSpec((tm, d), lambda i: (i, 0)),
        out_shape=jax.ShapeDtypeStruct((s, d), F32),
        scratch_shapes=[pltpu.VMEM((tm + POOL_HALO, d), F32)],
        compiler_params=_params("arbitrary"),
        name="pool_mixer",
    )(h, h, g.reshape(1, d), w_pool, scale.reshape(1, d))


MLP_PARTS = 2


def _mlp_kernel(*refs, final, n_side):
    x_ref, g_ref, win_ref, wout_ref, fg_ref = refs[:5]
    side_in = refs[5:5 + n_side]
    o_ref = refs[5 + n_side]
    side_out = refs[6 + n_side:6 + 2 * n_side]
    xn_ref = refs[6 + 2 * n_side]
    k = pl.program_id(1)

    @pl.when(k == 0)
    def _():
        x = x_ref[...]
        xn_ref[...] = _rms(x, g_ref[...]).astype(BF16)
        o_ref[...] = x

    xn = xn_ref[...]
    part = win_ref.shape[1] // MLP_PARTS
    us = []
    for p in range(MLP_PARTS):
        u = jnp.dot(xn, win_ref[:, p * part:(p + 1) * part], preferred_element_type=F32)
        u = jnp.maximum(u, 0.0)
        us.append((u * u).astype(BF16))
    o_ref[...] += jnp.dot(jnp.concatenate(us, axis=1), wout_ref[...], preferred_element_type=F32)

    for src, dst in zip(side_in, side_out):
        dst[...] = src[...].astype(BF16)

    if final:
        @pl.when(k == pl.num_programs(1) - 1)
        def _():
            o_ref[...] = _rms(o_ref[...], fg_ref[...])


def _mlp(h, g, w_in, w_out, final_g, side, *, final, tm=512, tf=1024):
    s, d = h.shape
    f = w_in.shape[1]
    nk = f // tf
    steps = (s // tm) * nk
    side_specs, side_out_specs, side_shapes = [], [], []
    for arr, layer in side:
        _, rows, cols = arr.shape
        rps = rows // steps
        side_specs.append(pl.BlockSpec((None, rps, cols), lambda i, k, layer=layer: (layer, i * nk + k, 0)))
        side_out_specs.append(pl.BlockSpec((rps, cols), lambda i, k: (i * nk + k, 0)))
        side_shapes.append(jax.ShapeDtypeStruct((rows, cols), BF16))
    outs = pl.pallas_call(
        functools.partial(_mlp_kernel, final=final, n_side=len(side)),
        grid=(s // tm, nk),
        in_specs=[
            pl.BlockSpec((tm, d), lambda i, k: (i, 0)),
            pl.BlockSpec((1, d), lambda i, k: (0, 0)),
            pl.BlockSpec((d, tf), lambda i, k: (0, k)),
            pl.BlockSpec((tf, d), lambda i, k: (k, 0)),
            pl.BlockSpec((1, d), lambda i, k: (0, 0)),
        ] + side_specs,
        out_specs=[pl.BlockSpec((tm, d), lambda i, k: (i, 0))] + side_out_specs,
        out_shape=[jax.ShapeDtypeStruct((s, d), F32)] + side_shapes,
        scratch_shapes=[pltpu.VMEM((tm, d), BF16)],
        compiler_params=_params("arbitrary", "arbitrary"),
        name="sqrelu_mlp",
    )(h, g.reshape(1, d), w_in, w_out, final_g.reshape(1, d), *[arr for arr, _ in side])
    return outs[0], outs[1:]


def _proj_kernel(x_ref, g_ref, w_ref, c_ref, a_ref, b_ref, *out_refs, modes):
    xn = _rms(x_ref[...], g_ref[...]).astype(BF16)
    n = D_MODEL
    for jj, (mode, o_ref) in enumerate(zip(modes, out_refs)):
        y = jnp.dot(xn, w_ref[:, jj * n:(jj + 1) * n], preferred_element_type=F32)
        for s in range(n // LANES):
            slab = slice(s * LANES, (s + 1) * LANES)
            t = y[:, slab]
            if mode != "tile_t":
                t = (t * c_ref[...] + pltpu.roll(t, LANES - ROT_DIM // 2, 1) * a_ref[...]
                     + pltpu.roll(t, ROT_DIM // 2, 1) * b_ref[...])
            if mode == "rope":
                o_ref[:, slab] = t.astype(BF16)
            elif mode == "rope_t":
                o_ref[slab, :] = t.T.astype(BF16)
            else:
                tt = t.T.astype(BF16)
                for r in range(o_ref.shape[0]):
                    o_ref[r, slab, :] = tt[:, r * ATTN_KEY_TILE:(r + 1) * ATTN_KEY_TILE]


def _proj(h, g, w, layer, tabs, modes, *, tm=512):
    s, d = h.shape
    n = D_MODEL
    out_specs, out_shapes = [], []
    for mode in modes:
        if mode == "rope":
            out_specs.append(pl.BlockSpec((tm, n), lambda i: (i, 0)))
            out_shapes.append(jax.ShapeDtypeStruct((s, n), BF16))
        elif mode == "rope_t":
            out_specs.append(pl.BlockSpec((n, tm), lambda i: (0, i)))
            out_shapes.append(jax.ShapeDtypeStruct((n, s), BF16))
        else:
            out_specs.append(pl.BlockSpec((tm // ATTN_KEY_TILE, n, ATTN_KEY_TILE), lambda i: (i, 0, 0)))
            out_shapes.append(jax.ShapeDtypeStruct((s // ATTN_KEY_TILE, n, ATTN_KEY_TILE), BF16))
    tab_spec = pl.BlockSpec((tm, LANES), lambda i: (i, 0))
    return pl.pallas_call(
        functools.partial(_proj_kernel, modes=tuple(modes)),
        grid=(s // tm,),
        in_specs=[
            pl.BlockSpec((tm, d), lambda i: (i, 0)),
            pl.BlockSpec((1, d), lambda i: (0, 0)),
            pl.BlockSpec((None, d, n * len(modes)), lambda i: (layer, 0, 0), pipeline_mode=pl.Buffered(1)),
            tab_spec, tab_spec, tab_spec,
        ],
        out_specs=out_specs,
        out_shape=out_shapes,
        compiler_params=_params("arbitrary"),
        name="norm_proj",
    )(h, g.reshape(1, d), w, *tabs)


def _rope_tables(seq, scale):
    half = ROT_DIM // 2
    pos = jnp.arange(seq, dtype=F32)
    inv_freq = ROPE_THETA ** (-jnp.arange(0, ROT_DIM, 2, dtype=F32) / ROT_DIM)
    ang = pos[:, None] * inv_freq[None, :]
    cos, sin = jnp.cos(ang), jnp.sin(ang)
    zeros = jnp.zeros((seq, half), F32)
    keep = LANES - ROT_DIM
    c = jnp.concatenate([cos, cos, jnp.ones((seq, keep), F32)], axis=1)
    a = jnp.concatenate([-sin, zeros, jnp.zeros((seq, keep), F32)], axis=1)
    b = jnp.concatenate([zeros, sin, jnp.zeros((seq, keep), F32)], axis=1)
    return c * scale, a * scale, b * scale


ATTN_Q_TILE = 2048
ATTN_KEY_BLOCK = 1024
ATTN_Q_SUB = 512
ATTN_KEY_TILE = 512


def _attn_kernel(qT_ref, k_ref, vT_ref, lam_ref, sg_ref, o_ref, m_ref, l_ref, acc_ref, s_ref, *, lam_init):
    t, kblock, sub, kt = ATTN_Q_TILE, ATTN_KEY_BLOCK, ATTN_Q_SUB, ATTN_KEY_TILE
    qi = pl.program_id(1)
    n_full = qi * (t // kblock)

    def init():
        m_ref[...] = jnp.full(m_ref.shape, MASK_VALUE, F32)
        l_ref[...] = jnp.zeros(l_ref.shape, F32)
        acc_ref[...] = jnp.zeros(acc_ref.shape, F32)

    def full_chains(kb):
        return [(c, q0, kblock // kt, kb * (kblock // kt), False) for q0 in range(0, t, sub) for c in range(2)]

    def diag_chains():
        return [(c, q0, (q0 + sub) // kt, qi * (t // kt), True) for q0 in range(0, t, sub) for c in range(2)]

    def score_tile(chain, j, slot):
        c, q0, _, tile0, masked = chain
        hd = slice(c * HEAD_DIM, (c + 1) * HEAD_DIM)
        k = k_ref[pl.ds(pl.multiple_of((tile0 + j) * kt, kt), kt), hd]
        s = jnp.dot(k, qT_ref[hd, q0:q0 + sub], preferred_element_type=F32)
        if masked and (j + 1) * kt > q0:
            kc = (j * kt + lax.broadcasted_iota(jnp.int32, (kt, sub), 0)) // CHUNK
            qc = (q0 + lax.broadcasted_iota(jnp.int32, (kt, sub), 1)) // CHUNK
            s = jnp.where(kc <= qc, s, MASK_VALUE)
        s_ref[slot, j * kt:(j + 1) * kt, :] = s
        return jnp.max(s, axis=0, keepdims=True)

    def value_tile(chain, j, slot, m_new):
        p = jnp.exp2(s_ref[slot, j * kt:(j + 1) * kt, :] - m_new)
        pv = jnp.dot(vT_ref[chain[3] + j], p.astype(BF16), preferred_element_type=F32)
        return jnp.sum(p, axis=0, keepdims=True), pv

    def sweep(chains, maxes, trailing=None):
        seq = chains + ([trailing] if trailing is not None else [])
        for i, chain in enumerate(chains):
            c, q0, n, _, _ = chain
            qs = slice(q0, q0 + sub)
            m_old = m_ref[c, :, qs]
            m_new = functools.reduce(jnp.maximum, maxes, m_old)
            alpha = jnp.exp2(m_old - m_new)
            nxt = seq[i + 1] if i + 1 < len(seq) else None
            n_nxt = nxt[2] if nxt is not None else 0
            maxes, l_new, acc = [], alpha * l_ref[c, :, qs], None
            lead = 0 if nxt is not None and not chain[4] and not nxt[4] else 1
            for j in range(n + 1):
                lead_end = n_nxt if j == n else min(n_nxt, j + lead + 1)
                for js in range(len(maxes), lead_end):
                    maxes.append(score_tile(nxt, js, (i + 1) % 2))
                if j < n:
                    l_j, pv = value_tile(chain, j, i % 2, m_new)
                    l_new = l_new + l_j
                    acc = alpha * acc_ref[c, :, qs] + pv if acc is None else acc + pv
            l_ref[c, :, qs] = l_new
            acc_ref[c, :, qs] = acc
            m_ref[c, :, qs] = m_new
        return maxes

    def first_scores(chain):
        return [score_tile(chain, j, 0) for j in range(chain[2])]

    def finalize():
        lam = lam_ref[...]
        lam_full = (jnp.exp(jnp.sum(lam[0:1] * lam[1:2], axis=1, keepdims=True))
                    - jnp.exp(jnp.sum(lam[2:3] * lam[3:4], axis=1, keepdims=True)) + lam_init)
        oT = acc_ref[0] * (1.0 / l_ref[0]) - acc_ref[1] * (lam_full / l_ref[1])
        inv_rms = lax.rsqrt(jnp.mean(oT * oT, axis=0, keepdims=True) + EPS) * (1.0 - lam_init)
        o_ref[...] = (oT * inv_rms * sg_ref[...]).astype(BF16)

    @pl.when(qi == 0)
    def _():
        init()
        chains = diag_chains()
        sweep(chains, first_scores(chains[0]))
        finalize()

    @pl.when(qi > 0)
    def _():
        init()

        def body(kb, maxes):
            return tuple(sweep(full_chains(kb), list(maxes), trailing=full_chains(kb + 1)[0]))

        maxes = lax.fori_loop(0, n_full - 1, body, tuple(first_scores(full_chains(0)[0])))
        sweep(full_chains(n_full - 1) + diag_chains(), list(maxes))
        finalize()


def _diff_attention(qT, k, vT, lam, subln_g, *, lam_init):
    s = k.shape[0]
    t, sub, kt = ATTN_Q_TILE, ATTN_Q_SUB, ATTN_KEY_TILE
    return pl.pallas_call(
        functools.partial(_attn_kernel, lam_init=lam_init),
        grid=(N_HEADS, s // t),
        in_specs=[
            pl.BlockSpec((V_HEAD_DIM, t), lambda h, i: (h, i)),
            pl.BlockSpec((s, V_HEAD_DIM), lambda h, i: (0, h)),
            pl.BlockSpec((s // kt, V_HEAD_DIM, kt), lambda h, i: (0, h, 0)),
            pl.BlockSpec((4, HEAD_DIM), lambda h, i: (0, 0)),
            pl.BlockSpec((V_HEAD_DIM, 1), lambda h, i: (0, 0)),
        ],
        out_specs=pl.BlockSpec((V_HEAD_DIM, t), lambda h, i: (h, i)),
        out_shape=jax.ShapeDtypeStruct((D_MODEL, s), BF16),
        scratch_shapes=[
            pltpu.VMEM((2, 1, t), F32),
            pltpu.VMEM((2, 1, t), F32),
            pltpu.VMEM((2, V_HEAD_DIM, t), F32),
            pltpu.VMEM((2, t, sub), F32),
        ],
        compiler_params=_params("arbitrary", "arbitrary"),
        name="diff_attention",
    )(qT, k, vT, lam, subln_g.reshape(V_HEAD_DIM, 1))


def _oproj_kernel(oT_ref, w_ref, h_ref, out_ref):
    y = lax.dot_general(oT_ref[...], w_ref[...], (((0,), (0,)), ((), ())), preferred_element_type=F32)
    out_ref[...] = h_ref[...] + y


def _oproj(o, w, layer, h, *, tm=512, tn=2048):
    s, d = h.shape
    return pl.pallas_call(
        _oproj_kernel,
        grid=(s // tm, d // tn),
        in_specs=[
            pl.BlockSpec((d, tm), lambda i, j: (0, i)),
            pl.BlockSpec((None, d, tn), lambda i, j: (layer, 0, j)),
            pl.BlockSpec((tm, tn), lambda i, j: (i, j)),
        ],
        out_specs=pl.BlockSpec((tm, tn), lambda i, j: (i, j)),
        out_shape=jax.ShapeDtypeStruct((s, d), F32),
        compiler_params=_params("arbitrary", "arbitrary"),
        name="attn_out_proj",
    )(o, w, h)


CAST_STEPS = 16


def _cast_kernel(*refs):
    n = len(refs) // 2
    for w_ref, o_ref in zip(refs[:n], refs[n:]):
        o_ref[...] = w_ref[...].astype(BF16)


def _to_bf16(items):
    in_specs, out_specs, out_shapes = [], [], []
    for w, layer in items:
        _, rows, cols = w.shape
        rb = rows // CAST_STEPS
        in_specs.append(pl.BlockSpec((None, rb, cols), lambda i, layer=layer: (layer, i, 0)))
        out_specs.append(pl.BlockSpec((rb, cols), lambda i: (i, 0)))
        out_shapes.append(jax.ShapeDtypeStruct((rows, cols), BF16))
    return pl.pallas_call(
        _cast_kernel,
        grid=(CAST_STEPS,),
        in_specs=in_specs,
        out_specs=out_specs,
        out_shape=out_shapes,
        compiler_params=_params("arbitrary"),
        name="cast_bf16",
    )(*[w for w, _ in items])


def kernel(x, mix_norm, mlp_norm, pool_w, pool_scale, kv_norm, w_kv, w_q, lam, subln, w_o,
           w_mlp_in, w_mlp_out, final_norm):
    b, s, d = x.shape
    assert (b, s, d) == (1, SEQ, D_MODEL)
    h = x.reshape(s, d)
    k_tabs = _rope_tables(s, 1.0)
    q_tabs = _rope_tables(s, HEAD_DIM ** -0.5 * math.log2(math.e))
    n_b = DEPTH - N_A_LAYERS
    pool16, w_in16, w_out16 = _to_bf16([(pool_w.reshape(1, -1, POOL_GROUP), 0), (w_mlp_in, 0), (w_mlp_out, 0)])
    pool16 = pool16.reshape(pool_w.shape)
    side_by_layer = {
        0: [(w_kv[None], 0), (w_q.reshape(1, n_b * d, d), 0)],
        1: [(w_o.reshape(1, n_b * d, d), 0)],
    }
    w_kv16 = w_q16 = w_o16 = k = vT = None
    for l in range(DEPTH):
        if l < N_A_LAYERS:
            h = _mixer(h, mix_norm[l], pool16, l, pool_scale[l])
        else:
            j = l - N_A_LAYERS
            lam_init = 0.8 - 0.6 * math.exp(-0.3 * l)
            qT, = _proj(h, mix_norm[l], w_q16, j, q_tabs, ["rope_t"])
            o = _diff_attention(qT, k, vT, lam[j], subln[j], lam_init=lam_init)
            h = _oproj(o, w_o16, j, h)
        side = [(w_mlp_in, l + 1), (w_mlp_out, l + 1)] if l + 1 < DEPTH else []
        side += side_by_layer.get(l, [])
        h, cast = _mlp(h, mlp_norm[l], w_in16, w_out16, final_norm, side, final=(l == DEPTH - 1))
        if l + 1 < DEPTH:
            w_in16, w_out16 = cast[0], cast[1]
        if l == 0:
            w_kv16, w_q16 = cast[2][None], cast[3].reshape(n_b, d, d)
        if l == 1:
            w_o16 = cast[2].reshape(n_b, d, d)
        if l == N_A_LAYERS - 1:
            k, vT = _proj(h, kv_norm, w_kv16, 0, k_tabs, ["rope", "tile_t"])
    return h.reshape(b, s, d)
```
